```python
import jax, jax.numpy as jnp
from jax import lax
import numpy as np

D_MODEL = 2048
BATCH = 4
SEQ = 2048
DEPTH = 2
DEC_BATCH = 8
DEC_SEQ = 8
PAST_LEN = 16384
PAGE_SIZE = 128

N_MIXERS = 2
N_A = (DEPTH + 1) // 2
N_B = DEPTH // 2
MH = 8
DQK = D_MODEL // 2 // MH
DV = D_MODEL // MH
QK_TOT = MH * DQK
V_TOT = MH * DV
PA = 2 * QK_TOT + 2 * V_TOT + 2 * MH
MLSTM_CHUNK = 128
SB_HEADS = 16
SB_HD = D_MODEL // SB_HEADS
SB_BLOCK = 128
SB_SCALE = SB_HD ** -0.5
SB_BIAS_INIT = -9.0
D_FF = -(-8 * D_MODEL // (3 * 256)) * 256
EPS = 1e-6
N_PAGES = PAST_LEN // PAGE_SIZE
N_POOL = (DEC_BATCH * N_PAGES * 5) // 4

kernel_name = 'hybrid_mlstm_stickbreak_decoder_step'


def rmsnorm(x, g):
    xf = x.astype(jnp.float32)
    return (xf * lax.rsqrt(jnp.mean(xf * xf, axis=-1, keepdims=True) + EPS)).astype(x.dtype) * g


def modulate(x, g, shift, scale):
    return rmsnorm(x, g) * (1 + scale[:, None, :]) + shift[:, None, :]


def ada_params(c, w, b):
    return jnp.split(jax.nn.silu(c) @ w + b, 6, axis=-1)


def swiglu(h, w_in, w_out):
    g, u = jnp.split(h @ w_in, 2, axis=-1)
    return (jax.nn.silu(g) * u) @ w_out


def _mlstm_chunk(carry, xs):
    C, n, m = carry
    q, k, v, li, lf = xs
    L = q.shape[1]
    b = jnp.cumsum(lf, axis=1)
    m_t = b + jnp.maximum(m[:, None, :], lax.cummax(li - b, axis=1))
    d_inter = jnp.exp(b + m[:, None, :] - m_t)
    causal = jnp.tril(jnp.ones((L, L), dtype=bool))[None, :, :, None]
    log_d = b[:, :, None, :] - b[:, None, :, :] + li[:, None, :, :] - m_t[:, :, None, :]
    dmat = jnp.exp(jnp.where(causal, log_d, -jnp.inf))
    s = jnp.einsum('bthd,bshd->btsh', q, k) * dmat
    num = jnp.einsum('btsh,bshv->bthv', s, v) + d_inter[..., None] * jnp.einsum('bthd,bhdv->bthv', q, C)
    qn = jnp.sum(s, axis=2) + d_inter * jnp.einsum('bthd,bhd->bth', q, n)
    h = num / jnp.maximum(jnp.abs(qn), jnp.exp(-m_t))[..., None]
    m_end = m_t[:, -1]
    b_end = b[:, -1]
    w = jnp.exp(b_end[:, None, :] - b + li - m_end[:, None, :])
    carry_decay = jnp.exp(b_end + m - m_end)
    C_new = carry_decay[..., None, None] * C + jnp.einsum('bsh,bshd,bshv->bhdv', w, k, v)
    n_new = carry_decay[..., None] * n + jnp.einsum('bsh,bshd->bhd', w, k)
    return (C_new, n_new, m_end), h


def mlstm_mixer(h, C0, n0, m0, w_in, b_gate, g_head, w_out, chunk):
    bsz, L, _ = h.shape
    proj = (h @ w_in).astype(jnp.float32)
    o1 = QK_TOT
    o2 = 2 * QK_TOT
    o3 = o2 + V_TOT
    o4 = o3 + V_TOT
    o5 = o4 + MH
    q, k, v, og, gi, gf = jnp.split(proj, [o1, o2, o3, o4, o5], axis=-1)
    q = q.reshape(bsz, L, MH, DQK) * (DQK ** -0.5)
    k = k.reshape(bsz, L, MH, DQK)
    v = v.reshape(bsz, L, MH, DV)
    bg = b_gate.astype(jnp.float32)
    li = gi + bg[0]
    lf = jax.nn.log_sigmoid(gf + bg[1])
    nc = L // chunk

    def blocks(a):
        return a.reshape(bsz, nc, chunk, *a.shape[2:]).swapaxes(0, 1)

    carry0 = (C0.astype(jnp.float32), n0.astype(jnp.float32), m0.astype(jnp.float32))
    (C, n, m), hs = lax.scan(_mlstm_chunk, carry0, (blocks(q), blocks(k), blocks(v), blocks(li), blocks(lf)))
    hs = hs.swapaxes(0, 1).reshape(bsz, L, MH, DV)
    hs = hs * lax.rsqrt(jnp.mean(hs * hs, axis=-1, keepdims=True) + EPS) * g_head.reshape(MH, DV)
    out = (hs.reshape(bsz, L, V_TOT) * jax.nn.sigmoid(og)).astype(h.dtype) @ w_out
    return out.astype(h.dtype), C, n, m


def sb_project(h, w_in):
    bsz, L, _ = h.shape
    proj = (h @ w_in).astype(jnp.float32).reshape(bsz, L, 3, SB_HEADS, SB_HD)
    return proj[:, :, 0], proj[:, :, 1], proj[:, :, 2]


def sb_attend(q, k, v, qpos, bias):
    z = jnp.einsum('bqhd,bkhd->bhqk', q, k.astype(jnp.float32)) * SB_SCALE
    z = z + bias.astype(jnp.float32)[None, :, None, None]
    kpos = jnp.arange(k.shape[1])
    mask = kpos[None, :] < qpos[:, None]
    log_beta = jax.nn.log_sigmoid(z)
    log_1mb = jnp.where(mask, jax.nn.log_sigmoid(-z), 0.0)
    suffix = lax.cumsum(log_1mb, axis=3, reverse=True) - log_1mb
    a = jnp.where(mask, jnp.exp(log_beta + suffix), 0.0)
    return jnp.einsum('bhqk,bkhd->bqhd', a, v.astype(jnp.float32))


def sb_prompt(q, k, v, bias):
    bsz, L = q.shape[0], q.shape[1]
    nblk = L // SB_BLOCK
    qb = q.reshape(bsz, nblk, SB_BLOCK, SB_HEADS, SB_HD).swapaxes(0, 1)
    pos = jnp.arange(L).reshape(nblk, SB_BLOCK)
    ob = lax.map(lambda a: sb_attend(a[0], k, v, a[1], bias), (qb, pos))
    return ob.swapaxes(0, 1).reshape(bsz, L, SB_HEADS * SB_HD)


def gather_pages(pool, page_table):
    rows = jnp.take(pool, page_table, axis=0)
    return rows.reshape(page_table.shape[0], page_table.shape[1] * PAGE_SIZE, SB_HEADS, SB_HD)


def setup_inputs(seed: int = 0) -> dict:
    key = jax.random.key(seed)
    ks = jax.random.split(key, 24)
    f32 = jnp.float32

    def nrm(k, shape, s):
        return jax.random.normal(k, shape, f32) * s

    pool_perm = jax.random.permutation(ks[7], N_POOL)
    page_table = pool_perm[:DEC_BATCH * N_PAGES].reshape(DEC_BATCH, N_PAGES).astype(jnp.int32)
    b_gate_a = jnp.stack([nrm(ks[13], (N_A, MH), 0.1), 3.0 + nrm(ks[14], (N_A, MH), 0.5)], axis=1)
    return {
        'x_prompt': nrm(ks[0], (BATCH, SEQ, D_MODEL), 1.0),
        'x_sample': nrm(ks[1], (DEC_BATCH, DEC_SEQ, D_MODEL), 1.0),
        'state_C': nrm(ks[2], (N_A, DEC_BATCH, MH, DQK, DV), 0.1),
        'state_n': nrm(ks[3], (N_A, DEC_BATCH, MH, DQK), 0.1),
        'state_m': nrm(ks[4], (N_A, DEC_BATCH, MH), 0.5),
        'cache_k': nrm(ks[5], (N_B, N_POOL, PAGE_SIZE, SB_HEADS, SB_HD), 1.0),
        'cache_v': nrm(ks[6], (N_B, N_POOL, PAGE_SIZE, SB_HEADS, SB_HD), 1.0),
        'page_table': page_table,
        'c_prompt': nrm(ks[8], (BATCH, D_MODEL), 1.0),
        'c_sample': nrm(ks[9], (DEC_BATCH, D_MODEL), 1.0),
        'w_ada': nrm(ks[10], (DEPTH, D_MODEL, 6 * D_MODEL), 0.5 * D_MODEL ** -0.5),
        'b_ada': nrm(ks[11], (DEPTH, 6 * D_MODEL), 0.02),
        'g_norm': 1.0 + nrm(ks[12], (DEPTH, 4, D_MODEL), 0.02),
        'w_in_a': nrm(ks[15], (N_A, D_MODEL, PA), D_MODEL ** -0.5),
        'b_gate_a': b_gate_a,
        'g_head_a': 1.0 + nrm(ks[16], (N_A, V_TOT), 0.02),
        'w_out_a': nrm(ks[17], (N_A, V_TOT, D_MODEL), V_TOT ** -0.5),
        'w_in_b': nrm(ks[18], (N_B, D_MODEL, 3 * SB_HEADS * SB_HD), D_MODEL ** -0.5),
        'b_sb': SB_BIAS_INIT + nrm(ks[22], (N_B, SB_HEADS), 0.3),
        'w_out_b': nrm(ks[19], (N_B, SB_HEADS * SB_HD, D_MODEL), (SB_HEADS * SB_HD) ** -0.5),
        'w_ffn_in': nrm(ks[20], (DEPTH, D_MODEL, 2 * D_FF), D_MODEL ** -0.5),
        'w_ffn_out': nrm(ks[21], (DEPTH, D_FF, D_MODEL), D_FF ** -0.5),
    }


def reference(x_prompt, x_sample, state_C, state_n, state_m, cache_k, cache_v, page_table, c_prompt, c_sample, w_ada, b_ada, g_norm, w_in_a, b_gate_a, g_head_a, w_out_a, w_in_b, b_sb, w_out_b, w_ffn_in, w_ffn_out):
    xp, xs = x_prompt, x_sample
    bp, bs = xp.shape[0], xs.shape[0]
    Cp_l, np_l, mp_l, Cs_l, ns_l, ms_l = [], [], [], [], [], []
    kp_l, vp_l, ks_l, vs_l = [], [], [], []
    pos_s = PAST_LEN + jnp.arange(DEC_SEQ)
    for i in range(DEPTH):
        j = i // N_MIXERS
        mp_ = ada_params(c_prompt, w_ada[i], b_ada[i])
        ms_ = ada_params(c_sample, w_ada[i], b_ada[i])
        hp = modulate(xp, g_norm[i, 0], mp_[0], mp_[1])
        hs = modulate(xs, g_norm[i, 0], ms_[0], ms_[1])
        if i % N_MIXERS == 0:
            zC = jnp.zeros((bp, MH, DQK, DV), jnp.float32)
            zn = jnp.zeros((bp, MH, DQK), jnp.float32)
            zm = jnp.zeros((bp, MH), jnp.float32)
            op, Cp, n_p, m_p = mlstm_mixer(hp, zC, zn, zm, w_in_a[j], b_gate_a[j], g_head_a[j], w_out_a[j], min(MLSTM_CHUNK, hp.shape[1]))
            os_, Cs, n_s, m_s = mlstm_mixer(hs, state_C[j], state_n[j], state_m[j], w_in_a[j], b_gate_a[j], g_head_a[j], w_out_a[j], hs.shape[1])
            Cp_l.append(Cp); np_l.append(n_p); mp_l.append(m_p)
            Cs_l.append(Cs); ns_l.append(n_s); ms_l.append(m_s)
        else:
            qp, kp, vp = sb_project(hp, w_in_b[j])
            op = (sb_prompt(qp, kp, vp, b_sb[j]).astype(xp.dtype) @ w_out_b[j]).astype(xp.dtype)
            qs, kn, vn = sb_project(hs, w_in_b[j])
            k_all = jnp.concatenate([gather_pages(cache_k[j], page_table).astype(jnp.float32), kn], axis=1)
            v_all = jnp.concatenate([gather_pages(cache_v[j], page_table).astype(jnp.float32), vn], axis=1)
            os_ = sb_attend(qs, k_all, v_all, pos_s, b_sb[j]).reshape(bs, xs.shape[1], SB_HEADS * SB_HD)
            os_ = (os_.astype(xs.dtype) @ w_out_b[j]).astype(xs.dtype)
            kp_l.append(kp.astype(xp.dtype)); vp_l.append(vp.astype(xp.dtype))
            ks_l.append(kn.astype(xs.dtype)); vs_l.append(vn.astype(xs.dtype))
        xp = xp + mp_[2][:, None, :] * rmsnorm(op, g_norm[i, 1])
        xs = xs + ms_[2][:, None, :] * rmsnorm(os_, g_norm[i, 1])
        hp = modulate(xp, g_norm[i, 2], mp_[3], mp_[4])
        hs = modulate(xs, g_norm[i, 2], ms_[3], ms_[4])
        xp = xp + mp_[5][:, None, :] * rmsnorm(swiglu(hp, w_ffn_in[i], w_ffn_out[i]), g_norm[i, 3])
        xs = xs + ms_[5][:, None, :] * rmsnorm(swiglu(hs, w_ffn_in[i], w_ffn_out[i]), g_norm[i, 3])
    k_prompt = jnp.stack(kp_l)
    v_prompt = jnp.stack(vp_l)
    k_sample = jnp.stack(ks_l)
    v_sample = jnp.stack(vs_l)
    C_prompt = jnp.stack(Cp_l)
    n_prompt = jnp.stack(np_l)
    m_prompt = jnp.stack(mp_l)
    C_sample = jnp.stack(Cs_l)
    n_sample = jnp.stack(ns_l)
    m_sample = jnp.stack(ms_l)
    return (xp, xs, k_prompt, v_prompt, k_sample, v_sample, C_prompt, n_prompt, m_prompt, C_sample, n_sample, m_sample)
```

```python
import functools

import jax
import jax.numpy as jnp
from jax import lax
from jax.experimental import pallas as pl
from jax.experimental.pallas import tpu as pltpu

F32 = jnp.float32
BF16 = jnp.bfloat16

EPS = 1e-6
MH = 8
MLSTM_CHUNK = 128
SB_HEADS = 16
SB_BLOCK = 128
PAGE_SIZE = 128
NEG_BIG = -1e30

LANES = 128
VMEM_LIMIT = 52 * 1024 * 1024


def _dot(a, b):
    return jnp.dot(a, b, preferred_element_type=F32)


def _dot_nt(a, b):
    return lax.dot_general(a, b, (((1,), (1,)), ((), ())), preferred_element_type=F32)


def _dot_tn(a, b):
    return lax.dot_general(a, b, (((0,), (0,)), ((), ())), preferred_element_type=F32)


def _params(sem):
    return pltpu.CompilerParams(dimension_semantics=sem, vmem_limit_bytes=VMEM_LIMIT)


def _ada_kernel(c_ref, w_ref, b_ref, o_ref):
    c = c_ref[...]
    a = (c * jax.nn.sigmoid(c)).astype(BF16)
    o_ref[0] = _dot(a, w_ref[0].astype(BF16)) + b_ref[0]


def _ada(c_all, w_ada, b_ada, tn=1024):
    depth, d, n = w_ada.shape
    rows = c_all.shape[0]
    return pl.pallas_call(
        _ada_kernel,
        grid=(depth, n // tn),
        in_specs=[
            pl.BlockSpec((rows, d), lambda l, j: (0, 0)),
            pl.BlockSpec((1, d, tn), lambda l, j: (l, 0, j)),
            pl.BlockSpec((1, 1, tn), lambda l, j: (l, 0, j)),
        ],
        out_specs=pl.BlockSpec((1, rows, tn), lambda l, j: (l, 0, j)),
        out_shape=jax.ShapeDtypeStruct((depth, rows, n), F32),
        compiler_params=_params(("parallel", "parallel")),
        name="ada_params",
    )(c_all, w_ada, b_ada.reshape(depth, 1, n))


def _row_chunks(tl):
    rc = min(tl, 256)
    return rc, tl // rc


def _modulated(x, g, scale, shift):
    r = lax.rsqrt(jnp.mean(x * x, axis=-1, keepdims=True) + EPS)
    return ((x * r) * g) * (1.0 + scale) + shift


def _modmm_kernel(*refs, n_slices, epilogue, has_gate, tl, per_row):
    x_ref, g_ref, sh_ref, sc_ref = refs[:4]
    w_refs = refs[4:4 + n_slices]
    pos = 4 + n_slices
    if has_gate:
        wg_ref = refs[pos]
        pos += 1
    n_out = len(refs) - pos - 1 - (1 if has_gate else 0)
    out_refs = refs[pos:pos + n_out]
    pos += n_out
    if has_gate:
        go_ref = refs[pos]
        pos += 1
    h_scr = refs[pos]
    j = pl.program_id(2)

    @pl.when(j == 0)
    def _():
        rc, n_chunks = _row_chunks(tl)

        def chunk(c):
            rows = pl.ds(pl.multiple_of(c * rc, rc), rc)
            x = x_ref[0, rows, :]
            if per_row:
                sc, sh = sc_ref[0, rows, :], sh_ref[0, rows, :]
            else:
                sc, sh = sc_ref[0], sh_ref[0]
            h_scr[rows, :] = _modulated(x, g_ref[0], sc, sh).astype(BF16)

        if n_chunks == 1:
            chunk(0)
        else:
            pl.loop(0, n_chunks)(chunk)
        if has_gate:
            go_ref[0] = _dot(h_scr[...], wg_ref[...].astype(BF16))

    hb = h_scr[...]
    ys = [_dot(hb, w_ref[...].astype(BF16)) for w_ref in w_refs]
    epilogue(ys, out_refs, j)


def _modmm(x, g, shift, scale, w, slice_starts, tn, n_tiles, out_dtypes, epilogue,
           w_gate=None, tl=1024):
    nb, L, D = x.shape
    tl = min(tl, L)
    per_row = shift.shape[1] != 1
    r = tl if per_row else 1
    n_slices = len(slice_starts)
    has_gate = w_gate is not None

    def mod_map(b, i, j):
        return (b, i if per_row else 0, 0)

    in_specs = [
        pl.BlockSpec((1, tl, D), lambda b, i, j: (b, i, 0)),
        pl.BlockSpec((1, 1, D), lambda b, i, j: (0, 0, 0)),
        pl.BlockSpec((1, r, D), mod_map),
        pl.BlockSpec((1, r, D), mod_map),
    ]
    args = [x, g.reshape(1, 1, D), shift, scale]
    for s0 in slice_starts:
        off = s0 // tn
        in_specs.append(pl.BlockSpec((D, tn), lambda b, i, j, off=off: (0, off + j)))
        args.append(w)
    if has_gate:
        gw = w_gate.shape[1]
        in_specs.append(pl.BlockSpec((D, gw), lambda b, i, j: (0, 0)))
        args.append(w_gate)
    out_specs = [pl.BlockSpec((1, tl, tn), lambda b, i, j: (b, i, j)) for _ in out_dtypes]
    out_shape = [jax.ShapeDtypeStruct((nb, L, n_tiles * tn), dt) for dt in out_dtypes]
    if has_gate:
        out_specs.append(pl.BlockSpec((1, tl, gw), lambda b, i, j: (b, i, 0)))
        out_shape.append(jax.ShapeDtypeStruct((nb, L, gw), F32))
    return pl.pallas_call(
        functools.partial(_modmm_kernel, n_slices=n_slices, epilogue=epilogue,
                          has_gate=has_gate, tl=tl, per_row=per_row),
        grid=(nb, L // tl, n_tiles),
        in_specs=in_specs,
        out_specs=out_specs,
        out_shape=out_shape,
        scratch_shapes=[pltpu.VMEM((tl, D), BF16)],
        compiler_params=_params(("parallel", "parallel", "arbitrary")),
        name="modulated_matmul",
    )(*args)


def _mlstm_proj_epilogue(ys, out_refs, j, *, q_tiles, q_scale):
    s = jnp.where(j < q_tiles, q_scale, 1.0).astype(F32)
    out_refs[0][0] = (ys[0] * s).astype(BF16)
    out_refs[1][0] = ys[1].astype(BF16)
    out_refs[2][0] = ys[2]


def _sb_proj_epilogue(ys, out_refs, j):
    out_refs[0][0] = ys[0].astype(BF16)
    out_refs[1][0] = ys[1]
    out_refs[2][0] = ys[2]


def _plain_epilogue(ys, out_refs, j):
    for y, out_ref in zip(ys, out_refs):
        out_ref[0] = y


def _swiglu_epilogue(ys, out_refs, j):
    gate, up = ys
    out_refs[0][0] = ((gate * jax.nn.sigmoid(gate)) * up).astype(BF16)


def _mm_norm_res_kernel(a_ref, w_ref, x_ref, gate_ref, g_ref, o_ref, *, tl, nk, per_row):
    k = pl.program_id(2)
    y = _dot(a_ref[0], w_ref[...].astype(BF16))

    @pl.when(k == 0)
    def _():
        o_ref[0] = y

    @pl.when(k > 0)
    def _():
        o_ref[0] += y

    @pl.when(k == nk - 1)
    def _():
        rc, n_chunks = _row_chunks(tl)

        def chunk(c):
            rows = pl.ds(pl.multiple_of(c * rc, rc), rc)
            acc = o_ref[0, rows, :]
            r = lax.rsqrt(jnp.mean(acc * acc, axis=-1, keepdims=True) + EPS)
            gate = gate_ref[0, rows, :] if per_row else gate_ref[0]
            o_ref[0, rows, :] = x_ref[0, rows, :] + gate * ((acc * r) * g_ref[0])

        if n_chunks == 1:
            chunk(0)
        else:
            pl.loop(0, n_chunks)(chunk)


def _mm_norm_res(a, w, x, gate, g, tk=512, tl=1024):
    nb, L, K = a.shape
    D = w.shape[1]
    tl = min(tl, L)
    per_row = gate.shape[1] != 1
    r = tl if per_row else 1
    nk = K // tk
    return pl.pallas_call(
        functools.partial(_mm_norm_res_kernel, tl=tl, nk=nk, per_row=per_row),
        grid=(nb, L // tl, nk),
        in_specs=[
            pl.BlockSpec((1, tl, tk), lambda b, i, k: (b, i, k)),
            pl.BlockSpec((tk, D), lambda b, i, k: (k, 0)),
            pl.BlockSpec((1, tl, D), lambda b, i, k: (b, i, 0)),
            pl.BlockSpec((1, r, D), lambda b, i, k: (b, i if per_row else 0, 0)),
            pl.BlockSpec((1, 1, D), lambda b, i, k: (0, 0, 0)),
        ],
        out_specs=pl.BlockSpec((1, tl, D), lambda b, i, k: (b, i, 0)),
        out_shape=jax.ShapeDtypeStruct((nb, L, D), F32),
        compiler_params=_params(("parallel", "parallel", "arbitrary")),
        name="matmul_norm_residual",
    )(a, w, x, gate, g.reshape(1, 1, D))


def _scan_rows(x, op, row):
    d = 1
    while d < x.shape[0]:
        shifted = pltpu.roll(x, d, axis=0)
        x = jnp.where(row >= d, op(x, shifted), x)
        d *= 2
    return x


def _mlstm_kernel(qk_ref, v_ref, og_ref, gt_ref, bias_ref, gh_ref, c0_ref, n0_ref, m0_ref,
                  o_ref, c_out_ref, n_out_ref, m_out_ref, c_scr, n_scr, m_scr,
                  *, valid_len, nc, dqk, dv):
    c = pl.program_id(1)
    L = MLSTM_CHUNK

    @pl.when(c == 0)
    def _():
        c_scr[...] = c0_ref[0]
        n_scr[...] = n0_ref[0]
        m_scr[...] = m0_ref[0]

    gates = gt_ref[0] + bias_ref[...]
    li = gates[:, :LANES]
    gf = gates[:, LANES:]
    lf = jnp.minimum(gf, 0.0) - jnp.log1p(jnp.exp(-jnp.abs(gf)))
    row = lax.broadcasted_iota(jnp.int32, (L, LANES), 0)
    col = lax.broadcasted_iota(jnp.int32, (L, LANES), 1)
    if valid_len < L:
        li = jnp.where(row < valid_len, li, NEG_BIG)
        lf = jnp.where(row < valid_len, lf, 0.0)

    b = _scan_rows(lf, jnp.add, row)
    u = li - b
    m_prev = m_scr[...]
    m_t = b + jnp.maximum(m_prev, _scan_rows(u, jnp.maximum, row))
    d_inter = jnp.exp(b + m_prev - m_t)
    ct = b - m_t
    inv_floor = jnp.exp(-m_t)
    b_end = b[L - 1:L, :]
    m_end = m_t[L - 1:L, :]
    w_all = jnp.exp(b_end - b + li - m_end)
    decay = jnp.exp(b_end + m_prev - m_end)
    u_t = u.T
    causal = row >= col

    for h in range(MH):
        q = qk_ref[0, :, h * dqk:(h + 1) * dqk]
        k = qk_ref[0, :, MH * dqk + h * dqk:MH * dqk + (h + 1) * dqk]
        v = v_ref[0, :, h * dv:(h + 1) * dv]
        log_d = jnp.where(causal, ct[:, h:h + 1] + u_t[h:h + 1, :], NEG_BIG)
        s = _dot_nt(q, k) * jnp.exp(log_d)
        di = d_inter[:, h:h + 1]
        c_prev = c_scr[h]
        num = _dot(s.astype(BF16), v) + di * _dot(q, c_prev.astype(BF16))
        n_prev = n_scr[h:h + 1, :]
        qn_state = jnp.sum(q.astype(F32) * n_prev.astype(BF16).astype(F32), axis=-1, keepdims=True)
        qn = jnp.sum(s, axis=-1, keepdims=True) + di * qn_state
        hh = num / jnp.maximum(jnp.abs(qn), inv_floor[:, h:h + 1])
        hh = hh * lax.rsqrt(jnp.mean(hh * hh, axis=-1, keepdims=True) + EPS)
        hh = hh * gh_ref[:, h * dv:(h + 1) * dv]
        og = og_ref[0, :, h * dv:(h + 1) * dv]
        o_ref[0, :, h * dv:(h + 1) * dv] = (hh * jax.nn.sigmoid(og)).astype(BF16)

        kw = k.astype(F32) * w_all[:, h:h + 1]
        dec = decay[:, h:h + 1]
        c_scr[h] = dec * c_prev + _dot_tn(kw.astype(BF16), v)
        n_scr[h:h + 1, :] = dec * n_prev + jnp.sum(kw, axis=0, keepdims=True)

    m_scr[...] = m_end

    @pl.when(c == nc - 1)
    def _():
        c_out_ref[0] = c_scr[...]
        n_out_ref[0] = n_scr[...]
        m_out_ref[0] = m_scr[...]


def _mlstm(qk, v, og, gates, bias_row, g_head, c0, n0, m0, valid_len):
    nb, L, vt = v.shape
    dv = vt // MH
    dqk = qk.shape[2] // (2 * MH)
    nc = L // MLSTM_CHUNK
    blk = lambda width: pl.BlockSpec((1, MLSTM_CHUNK, width), lambda b, c: (b, c, 0))
    return pl.pallas_call(
        functools.partial(_mlstm_kernel, valid_len=valid_len, nc=nc, dqk=dqk, dv=dv),
        grid=(nb, nc),
        in_specs=[
            blk(qk.shape[2]), blk(vt), blk(vt), blk(2 * LANES),
            pl.BlockSpec((1, 2 * LANES), lambda b, c: (0, 0)),
            pl.BlockSpec((1, vt), lambda b, c: (0, 0)),
            pl.BlockSpec((1, MH, dqk, dv), lambda b, c: (b, 0, 0, 0)),
            pl.BlockSpec((1, MH, dqk), lambda b, c: (b, 0, 0)),
            pl.BlockSpec((1, 1, LANES), lambda b, c: (b, 0, 0)),
        ],
        out_specs=[
            blk(vt),
            pl.BlockSpec((1, MH, dqk, dv), lambda b, c: (b, 0, 0, 0)),
            pl.BlockSpec((1, MH, dqk), lambda b, c: (b, 0, 0)),
            pl.BlockSpec((1, 1, LANES), lambda b, c: (b, 0, 0)),
        ],
        out_shape=[
            jax.ShapeDtypeStruct((nb, L, vt), BF16),
            jax.ShapeDtypeStruct((nb, MH, dqk, dv), F32),
            jax.ShapeDtypeStruct((nb, MH, dqk), F32),
            jax.ShapeDtypeStruct((nb, 1, LANES), F32),
        ],
        scratch_shapes=[
            pltpu.VMEM((MH, dqk, dv), F32),
            pltpu.VMEM((MH, dqk), F32),
            pltpu.VMEM((1, LANES), F32),
        ],
        compiler_params=_params(("parallel", "arbitrary")),
        name="mlstm_chunks",
    )(qk, v, og, gates, bias_row, g_head.reshape(1, vt), c0, n0, m0)


def _suffix_matrix():
    j = lax.broadcasted_iota(jnp.int32, (2 * SB_BLOCK, 2 * SB_BLOCK), 0) % SB_BLOCK
    s = lax.broadcasted_iota(jnp.int32, (2 * SB_BLOCK, 2 * SB_BLOCK), 1)
    return jnp.where((s >= SB_BLOCK) | (j > s), 1.0, 0.0).astype(BF16)


def _sb_block(z, v_bf, carry, suffix_mat, mask):
    soft = jnp.log1p(jnp.exp(-jnp.abs(z)))
    log_beta = jnp.minimum(z, 0.0) - soft
    log_1mb = log_beta - z
    if mask is not None:
        log_1mb = jnp.where(mask, log_1mb, 0.0)
    hi = log_1mb.astype(BF16)
    lo = (log_1mb - hi.astype(F32)).astype(BF16)
    sums = _dot(jnp.concatenate([hi, lo], axis=1), suffix_mat)
    a = jnp.exp(log_beta + sums[:, :SB_BLOCK] + carry)
    if mask is not None:
        a = jnp.where(mask, a, 0.0)
    a = a.astype(BF16)
    pv = v_bf(a) if callable(v_bf) else _dot(a, v_bf)
    return pv, carry + sums[:, SB_BLOCK:]


def _sb_prompt_kernel(q_ref, k_ref, v_ref, bias_ref, o_ref, k_scr, v_scr, *, nblk, scale):
    k_scr[...] = k_ref[0].astype(BF16)
    v_scr[...] = v_ref[0].astype(BF16)
    suffix_mat = _suffix_matrix()
    bias = bias_ref[...]
    row = lax.broadcasted_iota(jnp.int32, (SB_BLOCK, SB_BLOCK), 0)
    col = lax.broadcasted_iota(jnp.int32, (SB_BLOCK, SB_BLOCK), 1)
    strictly_causal = col < row

    def q_block(i):
        q = q_ref[0, pl.ds(pl.multiple_of(i * SB_BLOCK, SB_BLOCK), SB_BLOCK), :]

        def keys(j):
            rows = pl.ds(pl.multiple_of(j * SB_BLOCK, SB_BLOCK), SB_BLOCK)
            return k_scr[rows, :], v_scr[rows, :]

        kd, vd = keys(i)
        zero = jnp.zeros((SB_BLOCK, SB_BLOCK), F32)
        acc, carry = _sb_block(_dot_nt(q, kd) * scale + bias, vd, zero, suffix_mat, strictly_causal)

        def older(t, state):
            acc, carry = state
            kj, vj = keys(i - 1 - t)
            pv, carry = _sb_block(_dot_nt(q, kj) * scale + bias, vj, carry, suffix_mat, None)
            return acc + pv, carry

        acc, _ = lax.fori_loop(0, i, older, (acc, carry))
        o_ref[0, pl.ds(pl.multiple_of(i * SB_BLOCK, SB_BLOCK), SB_BLOCK), :] = acc.astype(BF16)

    pl.loop(0, nblk)(q_block)


def _sb_prompt(q, k, v, bias_lanes):
    nb, L, hd = q.shape
    dh = hd // SB_HEADS
    spec = pl.BlockSpec((1, L, dh), lambda b, h: (b, 0, h))
    return pl.pallas_call(
        functools.partial(_sb_prompt_kernel, nblk=L // SB_BLOCK, scale=dh ** -0.5),
        grid=(nb, SB_HEADS),
        in_specs=[spec, spec, spec, pl.BlockSpec((1, dh), lambda b, h: (0, h))],
        out_specs=spec,
        out_shape=jax.ShapeDtypeStruct((nb, L, hd), BF16),
        scratch_shapes=[pltpu.VMEM((L, dh), BF16), pltpu.VMEM((L, dh), BF16)],
        compiler_params=_params(("parallel", "parallel")),
        name="stickbreak_prompt",
    )(q, k, v, bias_lanes)


def _sb_decode_kernel(pt_ref, q_ref, bias_ref, kn_ref, vn_ref, kp_ref, vp_ref, o_ref,
                      carry_scr, acc_scr, *, n_pages, n_new, scale):
    p = pl.program_id(1)
    rows = SB_HEADS * n_new
    dh = q_ref.shape[2] // SB_HEADS
    row = lax.broadcasted_iota(jnp.int32, (rows, SB_BLOCK), 0)
    col = lax.broadcasted_iota(jnp.int32, (rows, SB_BLOCK), 1)
    row_head = row // n_new
    suffix_mat = _suffix_matrix()

    def head_rows(ref, h):
        return ref[0, pl.ds(h, SB_BLOCK, stride=SB_HEADS), :].astype(BF16)

    def attend(k_ref, v_ref, mask):
        q_rep = jnp.concatenate([q_ref[0]] * SB_HEADS, axis=0)
        lane_head = lax.broadcasted_iota(jnp.int32, q_rep.shape, 1) // dh
        row_h = lax.broadcasted_iota(jnp.int32, q_rep.shape, 0) // n_new
        q_bd = jnp.where(lane_head == row_h, q_rep, 0.0).astype(BF16)
        k_all = jnp.concatenate([head_rows(k_ref, h) for h in range(SB_HEADS)], axis=1)
        z = _dot_nt(q_bd, k_all) * scale + bias_ref[...]

        def values(a):
            out = jnp.zeros((rows, dh), F32)
            for h in range(SB_HEADS):
                out = out + jnp.where(row_head == h, _dot(a, head_rows(v_ref, h)), 0.0)
            return out

        pv, carry = _sb_block(z, values, carry_scr[...], suffix_mat, mask)
        acc_scr[...] += pv
        carry_scr[...] = carry

    @pl.when(p == 0)
    def _():
        carry_scr[...] = jnp.zeros_like(carry_scr)
        acc_scr[...] = jnp.zeros_like(acc_scr)
        attend(kn_ref, vn_ref, col < row % n_new)

    @pl.when(p > 0)
    def _():
        attend(kp_ref, vp_ref, None)

    @pl.when(p == n_pages)
    def _():
        o_ref[0] = acc_scr[...]


def _sb_decode(page_table, q, bias_rows, k_new, v_new, pool_k, pool_v):
    nb, n_new, hd = q.shape
    dh = hd // SB_HEADS
    n_pages = page_table.shape[1]
    rows = SB_HEADS * n_new
    prow = PAGE_SIZE * SB_HEADS

    def page_map(b, p, pt):
        return (pt[b, n_pages - jnp.maximum(p, 1)], 0, 0)

    grid_spec = pltpu.PrefetchScalarGridSpec(
        num_scalar_prefetch=1,
        grid=(nb, n_pages + 1),
        in_specs=[
            pl.BlockSpec((1, n_new, hd), lambda b, p, pt: (b, 0, 0)),
            pl.BlockSpec((rows, SB_BLOCK), lambda b, p, pt: (0, 0)),
            pl.BlockSpec((1, prow, dh), lambda b, p, pt: (b, 0, 0)),
            pl.BlockSpec((1, prow, dh), lambda b, p, pt: (b, 0, 0)),
            pl.BlockSpec((1, prow, dh), page_map),
            pl.BlockSpec((1, prow, dh), page_map),
        ],
        out_specs=pl.BlockSpec((1, rows, dh), lambda b, p, pt: (b, 0, 0)),
        scratch_shapes=[pltpu.VMEM((rows, SB_BLOCK), F32), pltpu.VMEM((rows, dh), F32)],
    )
    return pl.pallas_call(
        functools.partial(_sb_decode_kernel, n_pages=n_pages, n_new=n_new, scale=dh ** -0.5),
        grid_spec=grid_spec,
        out_shape=jax.ShapeDtypeStruct((nb, rows, dh), F32),
        compiler_params=_params(("parallel", "arbitrary")),
        name="stickbreak_decode",
    )(page_table, q, bias_rows, k_new, v_new, pool_k, pool_v)


def _ffn(x, mods, g_pre, g_post, w_in, w_out, tf=512):
    d_ff = w_out.shape[0]
    (act,) = _modmm(x, g_pre, mods[3], mods[4], w_in, (0, d_ff), tf, d_ff // tf,
                    (BF16,), _swiglu_epilogue)
    return _mm_norm_res(act, w_out, x, mods[5], g_post)


def _mlstm_layer(x, mods, g_norm, w_in, b_gate, g_head, w_out, c0, n0, m0, chunk_rows):
    D = x.shape[2]
    qk_tot = (w_in.shape[1] - 2 * MH) // 3
    dqk = qk_tot // (2 * MH)
    tn = 256
    w_gate = jnp.zeros((D, 2 * LANES), F32)
    w_gate = w_gate.at[:, :MH].set(w_in[:, 3 * qk_tot:3 * qk_tot + MH])
    w_gate = w_gate.at[:, LANES:LANES + MH].set(w_in[:, 3 * qk_tot + MH:])
    bias_row = jnp.zeros((1, 2 * LANES), F32)
    bias_row = bias_row.at[0, :MH].set(b_gate[0].astype(F32))
    bias_row = bias_row.at[0, LANES:LANES + MH].set(b_gate[1].astype(F32))
    epilogue = functools.partial(_mlstm_proj_epilogue, q_tiles=(MH * dqk) // tn, q_scale=dqk ** -0.5)
    qk, v, og, gates = _modmm(x, g_norm[0], mods[0], mods[1], w_in, (0, qk_tot, 2 * qk_tot), tn,
                              qk_tot // tn, (BF16, BF16, F32), epilogue, w_gate=w_gate)
    m0p = jnp.zeros((m0.shape[0], 1, LANES), F32).at[:, 0, :MH].set(m0.astype(F32))
    if chunk_rows == MLSTM_CHUNK:
        h, c_new, n_new, m_new = _mlstm(qk, v, og, gates, bias_row, g_head,
                                        c0.astype(F32), n0.astype(F32), m0p, MLSTM_CHUNK)
    else:
        nb = x.shape[1] // chunk_rows

        def pad(a):
            a = a.reshape(nb, chunk_rows, a.shape[2])
            return jnp.pad(a, ((0, 0), (0, MLSTM_CHUNK - chunk_rows), (0, 0)))

        h, c_new, n_new, m_new = _mlstm(pad(qk), pad(v), pad(og), pad(gates), bias_row, g_head,
                                        c0.astype(F32), n0.astype(F32), m0p, chunk_rows)
        h = h[:, :chunk_rows].reshape(1, nb * chunk_rows, h.shape[2])
    x = _mm_norm_res(h, w_out, x, mods[2], g_norm[1])
    return x, c_new, n_new, m_new[:, 0, :MH]


def kernel(x_prompt, x_sample, state_C, state_n, state_m, cache_k, cache_v, page_table, c_prompt, c_sample, w_ada, b_ada, g_norm, w_in_a, b_gate_a, g_head_a, w_out_a, w_in_b, b_sb, w_out_b, w_ffn_in, w_ffn_out):
    bp, seq, D = x_prompt.shape
    bs, dec_seq, _ = x_sample.shape
    depth = w_ada.shape[0]
    dh = D // SB_HEADS

    rows = -(-(bp + bs) // 8) * 8
    c_all = jnp.zeros((rows, D), F32).at[:bp].set(c_prompt).at[bp:bp + bs].set(c_sample)
    ada = _ada(c_all, w_ada, b_ada)

    xp = x_prompt
    xs = x_sample.reshape(1, bs * dec_seq, D)
    outs = {name: [] for name in ("kp", "vp", "ks", "vs", "Cp", "np", "mp", "Cs", "ns", "ms")}
    for i in range(depth):
        j = i // 2
        mods_p = [ada[i, :bp, s * D:(s + 1) * D].reshape(bp, 1, D) for s in range(6)]
        mods_s = [jnp.repeat(ada[i, bp:bp + bs, s * D:(s + 1) * D], dec_seq, axis=0).reshape(1, bs * dec_seq, D)
                  for s in range(6)]
        if i % 2 == 0:
            dqk = (w_in_a.shape[2] - 2 * MH) // 3 // (2 * MH)
            dv = 2 * dqk
            zc = jnp.zeros((bp, MH, dqk, dv), F32)
            zn = jnp.zeros((bp, MH, dqk), F32)
            zm = jnp.zeros((bp, MH), F32)
            xp, Cp, n_p, m_p = _mlstm_layer(xp, mods_p, g_norm[i], w_in_a[j], b_gate_a[j], g_head_a[j],
                                            w_out_a[j], zc, zn, zm, MLSTM_CHUNK)
            xs, Cs, n_s, m_s = _mlstm_layer(xs, mods_s, g_norm[i], w_in_a[j], b_gate_a[j], g_head_a[j],
                                            w_out_a[j], state_C[j], state_n[j], state_m[j], dec_seq)
            outs["Cp"].append(Cp); outs["np"].append(n_p); outs["mp"].append(m_p)
            outs["Cs"].append(Cs); outs["ns"].append(n_s); outs["ms"].append(m_s)
        else:
            tn = 256
            bias = b_sb[j].astype(F32)
            qp, kp, vp = _modmm(xp, g_norm[i, 0], mods_p[0], mods_p[1], w_in_b[j], (0, D, 2 * D), tn,
                                D // tn, (BF16, F32, F32), _sb_proj_epilogue)
            op = _sb_prompt(qp, kp, vp, jnp.repeat(bias, dh).reshape(1, D))
            xp = _mm_norm_res(op, w_out_b[j], xp, mods_p[2], g_norm[i, 1])
            qs, kn, vn = _modmm(xs, g_norm[i, 0], mods_s[0], mods_s[1], w_in_b[j], (0, D, 2 * D), tn,
                                D // tn, (F32, F32, F32), _plain_epilogue)

            def page_rows(a):
                a = a.reshape(bs, dec_seq, SB_HEADS, dh)
                a = jnp.pad(a, ((0, 0), (0, PAGE_SIZE - dec_seq), (0, 0), (0, 0)))
                return a.reshape(bs, PAGE_SIZE * SB_HEADS, dh)

            n_pool = cache_k.shape[1]
            os_ = _sb_decode(page_table, qs.reshape(bs, dec_seq, D),
                             jnp.broadcast_to(jnp.repeat(bias, dec_seq)[:, None], (SB_HEADS * dec_seq, SB_BLOCK)),
                             page_rows(kn), page_rows(vn),
                             cache_k[j].reshape(n_pool, PAGE_SIZE * SB_HEADS, dh),
                             cache_v[j].reshape(n_pool, PAGE_SIZE * SB_HEADS, dh))
            os_ = os_.reshape(bs, SB_HEADS, dec_seq, dh).transpose(0, 2, 1, 3).reshape(1, bs * dec_seq, D)
            xs = _mm_norm_res(os_.astype(BF16), w_out_b[j], xs, mods_s[2], g_norm[i, 1])
            outs["kp"].append(kp.reshape(bp, seq, SB_HEADS, dh)); outs["vp"].append(vp.reshape(bp, seq, SB_HEADS, dh))
            outs["ks"].append(kn.reshape(bs, dec_seq, SB_HEADS, dh)); outs["vs"].append(vn.reshape(bs, dec_seq, SB_HEADS, dh))
        xp = _ffn(xp, mods_p, g_norm[i, 2], g_norm[i, 3], w_ffn_in[i], w_ffn_out[i])
        xs = _ffn(xs, mods_s, g_norm[i, 2], g_norm[i, 3], w_ffn_in[i], w_ffn_out[i])

    st = lambda name: jnp.stack(outs[name])
    return (xp, xs.reshape(bs, dec_seq, D), st("kp"), st("vp"), st("ks"), st("vs"),
            st("Cp"), st("np"), st("mp"), st("Cs"), st("ns"), st("ms"))
```

```python
import functools

import jax
import jax.numpy as jnp
from jax import lax
from jax.experimental import pallas as pl
from jax.experimental.pallas import tpu as pltpu

F32 = jnp.float32
BF16 = jnp.bfloat16

EPS = 1e-6
MH = 8
MLSTM_CHUNK = 128
SB_HEADS = 16
SB_BLOCK = 128
SB_QTILE = 512
PAGE_SIZE = 128
NEG_BIG = -1e30

LANES = 128
VMEM_LIMIT = 52 * 1024 * 1024


def _dot(a, b):
    return jnp.dot(a, b, preferred_element_type=F32)


def _dot_nt(a, b):
    return lax.dot_general(a, b, (((1,), (1,)), ((), ())), preferred_element_type=F32)


def _dot_tn(a, b):
    return lax.dot_general(a, b, (((0,), (0,)), ((), ())), preferred_element_type=F32)


def _params(sem):
    return pltpu.CompilerParams(dimension_semantics=sem, vmem_limit_bytes=VMEM_LIMIT)


def _ada_kernel(c_ref, w_ref, b_ref, o_ref):
    c = c_ref[...]
    a = (c * jax.nn.sigmoid(c)).astype(BF16)
    o_ref[0] = _dot(a, w_ref[0].astype(BF16)) + b_ref[0]


def _ada(c_all, w_ada, b_ada, tn=1024):
    depth, d, n = w_ada.shape
    rows = c_all.shape[0]
    return pl.pallas_call(
        _ada_kernel,
        grid=(depth, n // tn),
        in_specs=[
            pl.BlockSpec((rows, d), lambda l, j: (0, 0)),
            pl.BlockSpec((1, d, tn), lambda l, j: (l, 0, j)),
            pl.BlockSpec((1, 1, tn), lambda l, j: (l, 0, j)),
        ],
        out_specs=pl.BlockSpec((1, rows, tn), lambda l, j: (l, 0, j)),
        out_shape=jax.ShapeDtypeStruct((depth, rows, n), F32),
        compiler_params=_params(("parallel", "parallel")),
        name="ada_params",
    )(c_all, w_ada, b_ada.reshape(depth, 1, n))


def _row_chunks(tl):
    rc = min(tl, 256)
    return rc, tl // rc


def _modulated(x, g, scale, shift):
    r = lax.rsqrt(jnp.mean(x * x, axis=-1, keepdims=True) + EPS)
    return ((x * r) * g) * (1.0 + scale) + shift


def _modmm_kernel(*refs, n_slices, epilogue, has_gate, tl, per_row):
    x_ref, g_ref, sh_ref, sc_ref = refs[:4]
    w_refs = refs[4:4 + n_slices]
    pos = 4 + n_slices
    if has_gate:
        wg_ref = refs[pos]
        pos += 1
    n_out = len(refs) - pos - 1 - (1 if has_gate else 0)
    out_refs = refs[pos:pos + n_out]
    pos += n_out
    if has_gate:
        go_ref = refs[pos]
        pos += 1
    h_scr = refs[pos]
    j = pl.program_id(2)

    @pl.when(j == 0)
    def _():
        rc, n_chunks = _row_chunks(tl)

        def chunk(c):
            rows = pl.ds(pl.multiple_of(c * rc, rc), rc)
            x = x_ref[0, rows, :]
            if per_row:
                sc, sh = sc_ref[0, rows, :], sh_ref[0, rows, :]
            else:
                sc, sh = sc_ref[0], sh_ref[0]
            h_scr[rows, :] = _modulated(x, g_ref[0], sc, sh).astype(BF16)

        if n_chunks == 1:
            chunk(0)
        else:
            pl.loop(0, n_chunks)(chunk)
        if has_gate:
            go_ref[0] = _dot(h_scr[...], wg_ref[...].astype(BF16))

    hb = h_scr[...]
    ys = [_dot(hb, w_ref[...].astype(BF16)) for w_ref in w_refs]
    epilogue(ys, out_refs, j)


def _modmm(x, g, shift, scale, w, slice_starts, tn, n_tiles, out_dtypes, epilogue,
           w_gate=None, tl=1024):
    nb, L, D = x.shape
    tl = min(tl, L)
    per_row = shift.shape[1] != 1
    r = tl if per_row else 1
    n_slices = len(slice_starts)
    has_gate = w_gate is not None

    def mod_map(b, i, j):
        return (b, i if per_row else 0, 0)

    in_specs = [
        pl.BlockSpec((1, tl, D), lambda b, i, j: (b, i, 0)),
        pl.BlockSpec((1, 1, D), lambda b, i, j: (0, 0, 0)),
        pl.BlockSpec((1, r, D), mod_map),
        pl.BlockSpec((1, r, D), mod_map),
    ]
    args = [x, g.reshape(1, 1, D), shift, scale]
    for s0 in slice_starts:
        off = s0 // tn
        in_specs.append(pl.BlockSpec((D, tn), lambda b, i, j, off=off: (0, off + j)))
        args.append(w)
    if has_gate:
        gw = w_gate.shape[1]
        in_specs.append(pl.BlockSpec((D, gw), lambda b, i, j: (0, 0)))
        args.append(w_gate)
    out_specs = [pl.BlockSpec((1, tl, tn), lambda b, i, j: (b, i, j)) for _ in out_dtypes]
    out_shape = [jax.ShapeDtypeStruct((nb, L, n_tiles * tn), dt) for dt in out_dtypes]
    if has_gate:
        out_specs.append(pl.BlockSpec((1, tl, gw), lambda b, i, j: (b, i, 0)))
        out_shape.append(jax.ShapeDtypeStruct((nb, L, gw), F32))
    return pl.pallas_call(
        functools.partial(_modmm_kernel, n_slices=n_slices, epilogue=epilogue,
                          has_gate=has_gate, tl=tl, per_row=per_row),
        grid=(nb, L // tl, n_tiles),
        in_specs=in_specs,
        out_specs=out_specs,
        out_shape=out_shape,
        scratch_shapes=[pltpu.VMEM((tl, D), BF16)],
        compiler_params=_params(("parallel", "parallel", "arbitrary")),
        name="modulated_matmul",
    )(*args)


def _mlstm_proj_epilogue(ys, out_refs, j, *, q_tiles, q_scale):
    s = jnp.where(j < q_tiles, q_scale, 1.0).astype(F32)
    out_refs[0][0] = (ys[0] * s).astype(BF16)
    out_refs[1][0] = ys[1].astype(BF16)
    out_refs[2][0] = ys[2]


def _sb_proj_epilogue(ys, out_refs, j):
    out_refs[0][0] = ys[0].astype(BF16)
    out_refs[1][0] = ys[1]
    out_refs[2][0] = ys[2]


def _plain_epilogue(ys, out_refs, j):
    for y, out_ref in zip(ys, out_refs):
        out_ref[0] = y


def _swiglu_epilogue(ys, out_refs, j):
    gate, up = ys
    out_refs[0][0] = ((gate * jax.nn.sigmoid(gate)) * up).astype(BF16)


def _mm_norm_res_kernel(a_ref, w_ref, x_ref, gate_ref, g_ref, o_ref, *, tl, nk, per_row):
    k = pl.program_id(2)
    y = _dot(a_ref[0], w_ref[...].astype(BF16))

    @pl.when(k == 0)
    def _():
        o_ref[0] = y

    @pl.when(k > 0)
    def _():
        o_ref[0] += y

    @pl.when(k == nk - 1)
    def _():
        rc, n_chunks = _row_chunks(tl)

        def chunk(c):
            rows = pl.ds(pl.multiple_of(c * rc, rc), rc)
            acc = o_ref[0, rows, :]
            r = lax.rsqrt(jnp.mean(acc * acc, axis=-1, keepdims=True) + EPS)
            gate = gate_ref[0, rows, :] if per_row else gate_ref[0]
            o_ref[0, rows, :] = x_ref[0, rows, :] + gate * ((acc * r) * g_ref[0])

        if n_chunks == 1:
            chunk(0)
        else:
            pl.loop(0, n_chunks)(chunk)


def _mm_norm_res(a, w, x, gate, g, tk=512, tl=1024):
    nb, L, K = a.shape
    D = w.shape[1]
    tl = min(tl, L)
    per_row = gate.shape[1] != 1
    r = tl if per_row else 1
    nk = K // tk
    return pl.pallas_call(
        functools.partial(_mm_norm_res_kernel, tl=tl, nk=nk, per_row=per_row),
        grid=(nb, L // tl, nk),
        in_specs=[
            pl.BlockSpec((1, tl, tk), lambda b, i, k: (b, i, k)),
            pl.BlockSpec((tk, D), lambda b, i, k: (k, 0)),
            pl.BlockSpec((1, tl, D), lambda b, i, k: (b, i, 0)),
            pl.BlockSpec((1, r, D), lambda b, i, k: (b, i if per_row else 0, 0)),
            pl.BlockSpec((1, 1, D), lambda b, i, k: (0, 0, 0)),
        ],
        out_specs=pl.BlockSpec((1, tl, D), lambda b, i, k: (b, i, 0)),
        out_shape=jax.ShapeDtypeStruct((nb, L, D), F32),
        compiler_params=_params(("parallel", "parallel", "arbitrary")),
        name="matmul_norm_residual",
    )(a, w, x, gate, g.reshape(1, 1, D))


def _scan_rows(x, op, row):
    d = 1
    while d < x.shape[0]:
        shifted = pltpu.roll(x, d, axis=0)
        x = jnp.where(row >= d, op(x, shifted), x)
        d *= 2
    return x


def _mlstm_kernel(qk_ref, v_ref, og_ref, gt_ref, bias_ref, gh_ref, c0_ref, n0_ref, m0_ref,
                  o_ref, c_out_ref, n_out_ref, m_out_ref, c_scr, n_scr, m_scr,
                  *, valid_len, nc, dqk, dv):
    c = pl.program_id(1)
    L = MLSTM_CHUNK

    @pl.when(c == 0)
    def _():
        c_scr[...] = c0_ref[0]
        n_scr[...] = n0_ref[0]
        m_scr[...] = m0_ref[0]

    gates = gt_ref[0] + bias_ref[...]
    li = gates[:, :LANES]
    gf = gates[:, LANES:]
    lf = jnp.minimum(gf, 0.0) - jnp.log1p(jnp.exp(-jnp.abs(gf)))
    row = lax.broadcasted_iota(jnp.int32, (L, LANES), 0)
    col = lax.broadcasted_iota(jnp.int32, (L, LANES), 1)
    if valid_len < L:
        li = jnp.where(row < valid_len, li, NEG_BIG)
        lf = jnp.where(row < valid_len, lf, 0.0)

    b = _scan_rows(lf, jnp.add, row)
    u = li - b
    m_prev = m_scr[...]
    m_t = b + jnp.maximum(m_prev, _scan_rows(u, jnp.maximum, row))
    d_inter = jnp.exp(b + m_prev - m_t)
    ct = b - m_t
    inv_floor = jnp.exp(-m_t)
    b_end = b[L - 1:L, :]
    m_end = m_t[L - 1:L, :]
    w_all = jnp.exp(b_end - b + li - m_end)
    decay = jnp.exp(b_end + m_prev - m_end)
    u_t = u.T
    causal = row >= col

    for h in range(MH):
        q = qk_ref[0, :, h * dqk:(h + 1) * dqk]
        k = qk_ref[0, :, MH * dqk + h * dqk:MH * dqk + (h + 1) * dqk]
        v = v_ref[0, :, h * dv:(h + 1) * dv]
        log_d = jnp.where(causal, ct[:, h:h + 1] + u_t[h:h + 1, :], NEG_BIG)
        s = _dot_nt(q, k) * jnp.exp(log_d)
        di = d_inter[:, h:h + 1]
        c_prev = c_scr[h]
        num = _dot(s.astype(BF16), v) + di * _dot(q, c_prev.astype(BF16))
        n_prev = n_scr[h:h + 1, :]
        qn_state = jnp.sum(q.astype(F32) * n_prev.astype(BF16).astype(F32), axis=-1, keepdims=True)
        qn = jnp.sum(s, axis=-1, keepdims=True) + di * qn_state
        hh = num / jnp.maximum(jnp.abs(qn), inv_floor[:, h:h + 1])
        hh = hh * lax.rsqrt(jnp.mean(hh * hh, axis=-1, keepdims=True) + EPS)
        hh = hh * gh_ref[:, h * dv:(h + 1) * dv]
        og = og_ref[0, :, h * dv:(h + 1) * dv]
        o_ref[0, :, h * dv:(h + 1) * dv] = (hh * jax.nn.sigmoid(og)).astype(BF16)

        kw = k.astype(F32) * w_all[:, h:h + 1]
        dec = decay[:, h:h + 1]
        c_scr[h] = dec * c_prev + _dot_tn(kw.astype(BF16), v)
        n_scr[h:h + 1, :] = dec * n_prev + jnp.sum(kw, axis=0, keepdims=True)

    m_scr[...] = m_end

    @pl.when(c == nc - 1)
    def _():
        c_out_ref[0] = c_scr[...]
        n_out_ref[0] = n_scr[...]
        m_out_ref[0] = m_scr[...]


def _mlstm(qk, v, og, gates, bias_row, g_head, c0, n0, m0, valid_len):
    nb, L, vt = v.shape
    dv = vt // MH
    dqk = qk.shape[2] // (2 * MH)
    nc = L // MLSTM_CHUNK
    blk = lambda width: pl.BlockSpec((1, MLSTM_CHUNK, width), lambda b, c: (b, c, 0))
    return pl.pallas_call(
        functools.partial(_mlstm_kernel, valid_len=valid_len, nc=nc, dqk=dqk, dv=dv),
        grid=(nb, nc),
        in_specs=[
            blk(qk.shape[2]), blk(vt), blk(vt), blk(2 * LANES),
            pl.BlockSpec((1, 2 * LANES), lambda b, c: (0, 0)),
            pl.BlockSpec((1, vt), lambda b, c: (0, 0)),
            pl.BlockSpec((1, MH, dqk, dv), lambda b, c: (b, 0, 0, 0)),
            pl.BlockSpec((1, MH, dqk), lambda b, c: (b, 0, 0)),
            pl.BlockSpec((1, 1, LANES), lambda b, c: (b, 0, 0)),
        ],
        out_specs=[
            blk(vt),
            pl.BlockSpec((1, MH, dqk, dv), lambda b, c: (b, 0, 0, 0)),
            pl.BlockSpec((1, MH, dqk), lambda b, c: (b, 0, 0)),
            pl.BlockSpec((1, 1, LANES), lambda b, c: (b, 0, 0)),
        ],
        out_shape=[
            jax.ShapeDtypeStruct((nb, L, vt), BF16),
            jax.ShapeDtypeStruct((nb, MH, dqk, dv), F32),
            jax.ShapeDtypeStruct((nb, MH, dqk), F32),
            jax.ShapeDtypeStruct((nb, 1, LANES), F32),
        ],
        scratch_shapes=[
            pltpu.VMEM((MH, dqk, dv), F32),
            pltpu.VMEM((MH, dqk), F32),
            pltpu.VMEM((1, LANES), F32),
        ],
        compiler_params=_params(("parallel", "arbitrary")),
        name="mlstm_chunks",
    )(qk, v, og, gates, bias_row, g_head.reshape(1, vt), c0, n0, m0)


def _suffix_matrix():
    j = lax.broadcasted_iota(jnp.int32, (2 * SB_BLOCK, 2 * SB_BLOCK), 0) % SB_BLOCK
    s = lax.broadcasted_iota(jnp.int32, (2 * SB_BLOCK, 2 * SB_BLOCK), 1)
    return jnp.where((s >= SB_BLOCK) | (j > s), 1.0, 0.0).astype(BF16)


def _sb_block(z, v_bf, carry, suffix_mat, mask):
    soft = jnp.log(1.0 + jnp.exp(-jnp.abs(z)))
    log_beta = jnp.minimum(z, 0.0) - soft
    log_1mb = log_beta - z
    if mask is not None:
        log_1mb = jnp.where(mask, log_1mb, 0.0)
    hi = log_1mb.astype(BF16)
    lo = (log_1mb - hi.astype(F32)).astype(BF16)
    sums = _dot(jnp.concatenate([hi, lo], axis=1), suffix_mat)
    a = jnp.exp(log_beta + sums[:, :SB_BLOCK] + carry)
    if mask is not None:
        a = jnp.where(mask, a, 0.0)
    a = a.astype(BF16)
    pv = v_bf(a) if callable(v_bf) else _dot(a, v_bf)
    return pv, carry + sums[:, SB_BLOCK:]


def _sb_prompt_kernel(q_ref, k_ref, v_ref, bias_ref, o_ref, k_scr, v_scr, acc_scr, carry_scr,
                      *, n_qt, scale):
    k_scr[...] = k_ref[0].astype(BF16)
    v_scr[...] = v_ref[0].astype(BF16)
    suffix_mat = _suffix_matrix()
    bias = bias_ref[...]
    sub = SB_QTILE // SB_BLOCK

    def visit(q, j, r0, mask):
        keys = pl.ds(pl.multiple_of(j * SB_BLOCK, SB_BLOCK), SB_BLOCK)
        z = _dot_nt(q, k_scr[keys, :]) * scale + bias
        pv, carry = _sb_block(z, v_scr[keys, :], carry_scr[r0:, :], suffix_mat, mask)
        acc_scr[r0:, :] += pv
        carry_scr[r0:, :] = carry

    for t in range(n_qt):
        q = q_ref[0, t * SB_QTILE:(t + 1) * SB_QTILE, :]
        acc_scr[...] = jnp.zeros_like(acc_scr)
        carry_scr[...] = jnp.zeros_like(carry_scr)
        for c in reversed(range(sub)):
            r0 = c * SB_BLOCK
            row = lax.broadcasted_iota(jnp.int32, (SB_QTILE - r0, SB_BLOCK), 0)
            col = lax.broadcasted_iota(jnp.int32, (SB_QTILE - r0, SB_BLOCK), 1)
            visit(q[r0:, :], t * sub + c, r0, col < row)

        def older(i, _, t=t, q=q):
            visit(q, t * sub - 1 - i, 0, None)
            return 0

        lax.fori_loop(0, t * sub, older, 0, unroll=4)
        o_ref[0, t * SB_QTILE:(t + 1) * SB_QTILE, :] = acc_scr[...].astype(BF16)


def _sb_prompt(q, k, v, bias_lanes):
    nb, L, hd = q.shape
    dh = hd // SB_HEADS
    spec = pl.BlockSpec((1, L, dh), lambda b, h: (b, 0, h))
    return pl.pallas_call(
        functools.partial(_sb_prompt_kernel, n_qt=L // SB_QTILE, scale=dh ** -0.5),
        grid=(nb, SB_HEADS),
        in_specs=[spec, spec, spec, pl.BlockSpec((1, dh), lambda b, h: (0, h))],
        out_specs=spec,
        out_shape=jax.ShapeDtypeStruct((nb, L, hd), BF16),
        scratch_shapes=[pltpu.VMEM((L, dh), BF16), pltpu.VMEM((L, dh), BF16),
                        pltpu.VMEM((SB_QTILE, dh), F32), pltpu.VMEM((SB_QTILE, SB_BLOCK), F32)],
        compiler_params=_params(("parallel", "parallel")),
        name="stickbreak_prompt",
    )(q, k, v, bias_lanes)


def _sb_decode_kernel(pt_ref, q_ref, bias_ref, kn_ref, vn_ref, kp_ref, vp_ref, o_ref,
                      carry_scr, acc_scr, *, n_pages, n_new, scale):
    p = pl.program_id(1)
    rows = SB_HEADS * n_new
    dh = q_ref.shape[2] // SB_HEADS
    row = lax.broadcasted_iota(jnp.int32, (rows, SB_BLOCK), 0)
    col = lax.broadcasted_iota(jnp.int32, (rows, SB_BLOCK), 1)
    row_head = row // n_new
    suffix_mat = _suffix_matrix()

    def head_rows(ref, h):
        return ref[0, pl.ds(h, SB_BLOCK, stride=SB_HEADS), :].astype(BF16)

    def attend(k_ref, v_ref, mask):
        q_rep = jnp.concatenate([q_ref[0]] * SB_HEADS, axis=0)
        lane_head = lax.broadcasted_iota(jnp.int32, q_rep.shape, 1) // dh
        row_h = lax.broadcasted_iota(jnp.int32, q_rep.shape, 0) // n_new
        q_bd = jnp.where(lane_head == row_h, q_rep, 0.0).astype(BF16)
        k_all = jnp.concatenate([head_rows(k_ref, h) for h in range(SB_HEADS)], axis=1)
        z = _dot_nt(q_bd, k_all) * scale + bias_ref[...]

        def values(a):
            out = jnp.zeros((rows, dh), F32)
            for h in range(SB_HEADS):
                out = out + jnp.where(row_head == h, _dot(a, head_rows(v_ref, h)), 0.0)
            return out

        pv, carry = _sb_block(z, values, carry_scr[...], suffix_mat, mask)
        acc_scr[...] += pv
        carry_scr[...] = carry

    @pl.when(p == 0)
    def _():
        carry_scr[...] = jnp.zeros_like(carry_scr)
        acc_scr[...] = jnp.zeros_like(acc_scr)
        attend(kn_ref, vn_ref, col < row % n_new)

    @pl.when(p > 0)
    def _():
        attend(kp_ref, vp_ref, None)

    @pl.when(p == n_pages)
    def _():
        o_ref[0] = acc_scr[...]


def _sb_decode(page_table, q, bias_rows, k_new, v_new, pool_k, pool_v):
    nb, n_new, hd = q.shape
    dh = hd // SB_HEADS
    n_pages = page_table.shape[1]
    rows = SB_HEADS * n_new
    prow = PAGE_SIZE * SB_HEADS

    def page_map(b, p, pt):
        return (pt[b, n_pages - jnp.maximum(p, 1)], 0, 0)

    grid_spec = pltpu.PrefetchScalarGridSpec(
        num_scalar_prefetch=1,
        grid=(nb, n_pages + 1),
        in_specs=[
            pl.BlockSpec((1, n_new, hd), lambda b, p, pt: (b, 0, 0)),
            pl.BlockSpec((rows, SB_BLOCK), lambda b, p, pt: (0, 0)),
            pl.BlockSpec((1, prow, dh), lambda b, p, pt: (b, 0, 0)),
            pl.BlockSpec((1, prow, dh), lambda b, p, pt: (b, 0, 0)),
            pl.BlockSpec((1, prow, dh), page_map),
            pl.BlockSpec((1, prow, dh), page_map),
        ],
        out_specs=pl.BlockSpec((1, rows, dh), lambda b, p, pt: (b, 0, 0)),
        scratch_shapes=[pltpu.VMEM((rows, SB_BLOCK), F32), pltpu.VMEM((rows, dh), F32)],
    )
    return pl.pallas_call(
        functools.partial(_sb_decode_kernel, n_pages=n_pages, n_new=n_new, scale=dh ** -0.5),
        grid_spec=grid_spec,
        out_shape=jax.ShapeDtypeStruct((nb, rows, dh), F32),
        compiler_params=_params(("parallel", "arbitrary")),
        name="stickbreak_decode",
    )(page_table, q, bias_rows, k_new, v_new, pool_k, pool_v)


def _ffn(x, mods, g_pre, g_post, w_in, w_out, tf=512):
    d_ff = w_out.shape[0]
    (act,) = _modmm(x, g_pre, mods[3], mods[4], w_in, (0, d_ff), tf, d_ff // tf,
                    (BF16,), _swiglu_epilogue)
    return _mm_norm_res(act, w_out, x, mods[5], g_post)


def _mlstm_layer(x, mods, g_norm, w_in, b_gate, g_head, w_out, c0, n0, m0, chunk_rows):
    D = x.shape[2]
    qk_tot = (w_in.shape[1] - 2 * MH) // 3
    dqk = qk_tot // (2 * MH)
    tn = 256
    w_gate = jnp.zeros((D, 2 * LANES), F32)
    w_gate = w_gate.at[:, :MH].set(w_in[:, 3 * qk_tot:3 * qk_tot + MH])
    w_gate = w_gate.at[:, LANES:LANES + MH].set(w_in[:, 3 * qk_tot + MH:])
    bias_row = jnp.zeros((1, 2 * LANES), F32)
    bias_row = bias_row.at[0, :MH].set(b_gate[0].astype(F32))
    bias_row = bias_row.at[0, LANES:LANES + MH].set(b_gate[1].astype(F32))
    epilogue = functools.partial(_mlstm_proj_epilogue, q_tiles=(MH * dqk) // tn, q_scale=dqk ** -0.5)
    qk, v, og, gates = _modmm(x, g_norm[0], mods[0], mods[1], w_in, (0, qk_tot, 2 * qk_tot), tn,
                              qk_tot // tn, (BF16, BF16, F32), epilogue, w_gate=w_gate)
    m0p = jnp.zeros((m0.shape[0], 1, LANES), F32).at[:, 0, :MH].set(m0.astype(F32))
    if chunk_rows == MLSTM_CHUNK:
        h, c_new, n_new, m_new = _mlstm(qk, v, og, gates, bias_row, g_head,
                                        c0.astype(F32), n0.astype(F32), m0p, MLSTM_CHUNK)
    else:
        nb = x.shape[1] // chunk_rows

        def pad(a):
            a = a.reshape(nb, chunk_rows, a.shape[2])
            return jnp.pad(a, ((0, 0), (0, MLSTM_CHUNK - chunk_rows), (0, 0)))

        h, c_new, n_new, m_new = _mlstm(pad(qk), pad(v), pad(og), pad(gates), bias_row, g_head,
                                        c0.astype(F32), n0.astype(F32), m0p, chunk_rows)
        h = h[:, :chunk_rows].reshape(1, nb * chunk_rows, h.shape[2])
    x = _mm_norm_res(h, w_out, x, mods[2], g_norm[1])
    return x, c_new, n_new, m_new[:, 0, :MH]


def kernel(x_prompt, x_sample, state_C, state_n, state_m, cache_k, cache_v, page_table, c_prompt, c_sample, w_ada, b_ada, g_norm, w_in_a, b_gate_a, g_head_a, w_out_a, w_in_b, b_sb, w_out_b, w_ffn_in, w_ffn_out):
    bp, seq, D = x_prompt.shape
    bs, dec_seq, _ = x_sample.shape
    depth = w_ada.shape[0]
    dh = D // SB_HEADS

    rows = -(-(bp + bs) // 8) * 8
    c_all = jnp.zeros((rows, D), F32).at[:bp].set(c_prompt).at[bp:bp + bs].set(c_sample)
    ada = _ada(c_all, w_ada, b_ada)

    xp = x_prompt
    xs = x_sample.reshape(1, bs * dec_seq, D)
    outs = {name: [] for name in ("kp", "vp", "ks", "vs", "Cp", "np", "mp", "Cs", "ns", "ms")}
    for i in range(depth):
        j = i // 2
        mods_p = [ada[i, :bp, s * D:(s + 1) * D].reshape(bp, 1, D) for s in range(6)]
        mods_s = [jnp.repeat(ada[i, bp:bp + bs, s * D:(s + 1) * D], dec_seq, axis=0).reshape(1, bs * dec_seq, D)
                  for s in range(6)]
        if i % 2 == 0:
            dqk = (w_in_a.shape[2] - 2 * MH) // 3 // (2 * MH)
            dv = 2 * dqk
            zc = jnp.zeros((bp, MH, dqk, dv), F32)
            zn = jnp.zeros((bp, MH, dqk), F32)
            zm = jnp.zeros((bp, MH), F32)
            xp, Cp, n_p, m_p = _mlstm_layer(xp, mods_p, g_norm[i], w_in_a[j], b_gate_a[j], g_head_a[j],
                                            w_out_a[j], zc, zn, zm, MLSTM_CHUNK)
            xs, Cs, n_s, m_s = _mlstm_layer(xs, mods_s, g_norm[i], w_in_a[j], b_gate_a[j], g_head_a[j],
                                            w_out_a[j], state_C[j], state_n[j], state_m[j], dec_seq)
            outs["Cp"].append(Cp); outs["np"].append(n_p); outs["mp"].append(m_p)
            outs["Cs"].append(Cs); outs["ns"].append(n_s); outs["ms"].append(m_s)
        else:
            tn = 256
            bias = b_sb[j].astype(F32)
            qp, kp, vp = _modmm(xp, g_norm[i, 0], mods_p[0], mods_p[1], w_in_b[j], (0, D, 2 * D), tn,
                                D // tn, (BF16, F32, F32), _sb_proj_epilogue)
            op = _sb_prompt(qp, kp, vp, jnp.repeat(bias, dh).reshape(1, D))
            xp = _mm_norm_res(op, w_out_b[j], xp, mods_p[2], g_norm[i, 1])
            qs, kn, vn = _modmm(xs, g_norm[i, 0], mods_s[0], mods_s[1], w_in_b[j], (0, D, 2 * D), tn,
                                D // tn, (F32, F32, F32), _plain_epilogue)

            def page_rows(a):
                a = a.reshape(bs, dec_seq, SB_HEADS, dh)
                a = jnp.pad(a, ((0, 0), (0, PAGE_SIZE - dec_seq), (0, 0), (0, 0)))
                return a.reshape(bs, PAGE_SIZE * SB_HEADS, dh)

            n_pool = cache_k.shape[1]
            os_ = _sb_decode(page_table, qs.reshape(bs, dec_seq, D),
                             jnp.broadcast_to(jnp.repeat(bias, dec_seq)[:, None], (SB_HEADS * dec_seq, SB_BLOCK)),
                             page_rows(kn), page_rows(vn),
                             cache_k[j].reshape(n_pool, PAGE_SIZE * SB_HEADS, dh),
                             cache_v[j].reshape(n_pool, PAGE_SIZE * SB_HEADS, dh))
            os_ = os_.reshape(bs, SB_HEADS, dec_seq, dh).transpose(0, 2, 1, 3).reshape(1, bs * dec_seq, D)
            xs = _mm_norm_res(os_.astype(BF16), w_out_b[j], xs, mods_s[2], g_norm[i, 1])
            outs["kp"].append(kp.reshape(bp, seq, SB_HEADS, dh)); outs["vp"].append(vp.reshape(bp, seq, SB_HEADS, dh))
            outs["ks"].append(kn.reshape(bs, dec_seq, SB_HEADS, dh)); outs["vs"].append(vn.reshape(bs, dec_seq, SB_HEADS, dh))
        xp = _ffn(xp, mods_p, g_norm[i, 2], g_norm[i, 3], w_ffn_in[i], w_ffn_out[i])
        xs = _ffn(xs, mods_s, g_norm[i, 2], g_norm[i, 3], w_ffn_in[i], w_ffn_out[i])

    st = lambda name: jnp.stack(outs[name])
    return (xp, xs.reshape(bs, dec_seq, D), st("kp"), st("vp"), st("ks"), st("vs"),
            st("Cp"), st("np"), st("mp"), st("Cs"), st("ns"), st("ms"))
```

```python
import functools

import jax
import jax.numpy as jnp
from jax import lax
from jax.experimental import pallas as pl
from jax.experimental.pallas import tpu as pltpu

F32 = jnp.float32
BF16 = jnp.bfloat16

EPS = 1e-6
MH = 8
MLSTM_CHUNK = 128
SB_HEADS = 16
SB_BLOCK = 128
SB_QTILE = 512
PAGE_SIZE = 128
NEG_BIG = -1e30

LANES = 128
SUBLANES = 8
VMEM_LIMIT = 52 * 1024 * 1024


def _dot(a, b):
    return jnp.dot(a, b, preferred_element_type=F32)


def _dot_nt(a, b):
    return lax.dot_general(a, b, (((1,), (1,)), ((), ())), preferred_element_type=F32)


def _dot_tn(a, b):
    return lax.dot_general(a, b, (((0,), (0,)), ((), ())), preferred_element_type=F32)


def _params(sem):
    return pltpu.CompilerParams(dimension_semantics=sem, vmem_limit_bytes=VMEM_LIMIT)


def _ada_kernel(c_ref, w_ref, b_ref, o_ref):
    c = c_ref[...]
    a = (c * jax.nn.sigmoid(c)).astype(BF16)
    o_ref[0] = _dot(a, w_ref[0].astype(BF16)) + b_ref[0]


def _ada(c_all, w_ada, b_ada, tn=1024):
    depth, d, n = w_ada.shape
    rows = c_all.shape[0]
    return pl.pallas_call(
        _ada_kernel,
        grid=(depth, n // tn),
        in_specs=[
            pl.BlockSpec((rows, d), lambda l, j: (0, 0)),
            pl.BlockSpec((1, d, tn), lambda l, j: (l, 0, j)),
            pl.BlockSpec((1, 1, tn), lambda l, j: (l, 0, j)),
        ],
        out_specs=pl.BlockSpec((1, rows, tn), lambda l, j: (l, 0, j)),
        out_shape=jax.ShapeDtypeStruct((depth, rows, n), F32),
        compiler_params=_params(("parallel", "parallel")),
        name="ada_params",
    )(c_all, w_ada, b_ada.reshape(depth, 1, n))


def _row_chunks(tl):
    rc = min(tl, 256)
    return rc, tl // rc


def _modulated(x, g, scale, shift):
    r = lax.rsqrt(jnp.mean(x * x, axis=-1, keepdims=True) + EPS)
    return ((x * r) * g) * (1.0 + scale) + shift


def _modmm_kernel(*refs, n_slices, epilogue, has_gate, tl, per_row):
    x_ref, g_ref, sh_ref, sc_ref = refs[:4]
    w_refs = refs[4:4 + n_slices]
    pos = 4 + n_slices
    if has_gate:
        wg_ref = refs[pos]
        pos += 1
    n_out = len(refs) - pos - 1 - (1 if has_gate else 0)
    out_refs = refs[pos:pos + n_out]
    pos += n_out
    if has_gate:
        go_ref = refs[pos]
        pos += 1
    h_scr = refs[pos]
    j = pl.program_id(2)

    @pl.when(j == 0)
    def _():
        rc, n_chunks = _row_chunks(tl)

        def chunk(c):
            rows = pl.ds(pl.multiple_of(c * rc, rc), rc)
            x = x_ref[0, rows, :]
            if per_row:
                sc, sh = sc_ref[0, rows, :], sh_ref[0, rows, :]
            else:
                sc, sh = sc_ref[0], sh_ref[0]
            h_scr[rows, :] = _modulated(x, g_ref[0], sc, sh).astype(BF16)

        if n_chunks == 1:
            chunk(0)
        else:
            pl.loop(0, n_chunks)(chunk)
        if has_gate:
            go_ref[0] = _dot(h_scr[...], wg_ref[...].astype(BF16))

    hb = h_scr[...]
    ys = [_dot(hb, w_ref[...].astype(BF16)) for w_ref in w_refs]
    epilogue(ys, out_refs, j)


def _modmm(x, g, shift, scale, w, layer, slice_starts, tn, n_tiles, out_dtypes, epilogue,
           w_gate=None, tl=1024):
    nb, L, D = x.shape
    tl = min(tl, L)
    per_row = shift.shape[1] != 1
    r = tl if per_row else 1
    n_slices = len(slice_starts)
    has_gate = w_gate is not None

    def mod_map(b, i, j):
        return (b, i if per_row else 0, 0)

    in_specs = [
        pl.BlockSpec((1, tl, D), lambda b, i, j: (b, i, 0)),
        pl.BlockSpec((1, 1, D), lambda b, i, j: (0, 0, 0)),
        pl.BlockSpec((1, r, D), mod_map),
        pl.BlockSpec((1, r, D), mod_map),
    ]
    args = [x, g.reshape(1, 1, D), shift, scale]
    for s0 in slice_starts:
        off = s0 // tn
        in_specs.append(pl.BlockSpec((None, D, tn), lambda b, i, j, off=off: (layer, 0, off + j)))
        args.append(w)
    if has_gate:
        gw = w_gate.shape[1]
        in_specs.append(pl.BlockSpec((D, gw), lambda b, i, j: (0, 0)))
        args.append(w_gate)
    out_specs = [pl.BlockSpec((1, tl, tn), lambda b, i, j: (b, i, j)) for _ in out_dtypes]
    out_shape = [jax.ShapeDtypeStruct((nb, L, n_tiles * tn), dt) for dt in out_dtypes]
    if has_gate:
        out_specs.append(pl.BlockSpec((1, tl, gw), lambda b, i, j: (b, i, 0)))
        out_shape.append(jax.ShapeDtypeStruct((nb, L, gw), F32))
    return pl.pallas_call(
        functools.partial(_modmm_kernel, n_slices=n_slices, epilogue=epilogue,
                          has_gate=has_gate, tl=tl, per_row=per_row),
        grid=(nb, L // tl, n_tiles),
        in_specs=in_specs,
        out_specs=out_specs,
        out_shape=out_shape,
        scratch_shapes=[pltpu.VMEM((tl, D), BF16)],
        compiler_params=_params(("parallel", "parallel", "arbitrary")),
        name="modulated_matmul",
    )(*args)


def _mlstm_proj_epilogue(ys, out_refs, j, *, q_tiles, q_scale):
    s = jnp.where(j < q_tiles, q_scale, 1.0).astype(F32)
    out_refs[0][0] = (ys[0] * s).astype(BF16)
    out_refs[1][0] = ys[1].astype(BF16)
    out_refs[2][0] = ys[2]


def _sb_proj_epilogue(ys, out_refs, j):
    out_refs[0][0] = ys[0].astype(BF16)
    out_refs[1][0] = ys[1]
    out_refs[2][0] = ys[2]


def _plain_epilogue(ys, out_refs, j):
    for y, out_ref in zip(ys, out_refs):
        out_ref[0] = y


def _swiglu_epilogue(ys, out_refs, j):
    gate, up = ys
    out_refs[0][0] = ((gate * jax.nn.sigmoid(gate)) * up).astype(BF16)


def _mm_norm_res_kernel(a_ref, w_ref, x_ref, gate_ref, g_ref, o_ref, *, tl, nk, per_row):
    k = pl.program_id(2)
    y = _dot(a_ref[0], w_ref[...].astype(BF16))

    @pl.when(k == 0)
    def _():
        o_ref[0] = y

    @pl.when(k > 0)
    def _():
        o_ref[0] += y

    @pl.when(k == nk - 1)
    def _():
        rc, n_chunks = _row_chunks(tl)

        def chunk(c):
            rows = pl.ds(pl.multiple_of(c * rc, rc), rc)
            acc = o_ref[0, rows, :]
            r = lax.rsqrt(jnp.mean(acc * acc, axis=-1, keepdims=True) + EPS)
            gate = gate_ref[0, rows, :] if per_row else gate_ref[0]
            o_ref[0, rows, :] = x_ref[0, rows, :] + gate * ((acc * r) * g_ref[0])

        if n_chunks == 1:
            chunk(0)
        else:
            pl.loop(0, n_chunks)(chunk)


def _mm_norm_res(a, w, layer, x, gate, g, tk=512, tl=1024):
    nb, L, K = a.shape
    D = w.shape[2]
    tl = min(tl, L)
    per_row = gate.shape[1] != 1
    r = tl if per_row else 1
    nk = K // tk
    return pl.pallas_call(
        functools.partial(_mm_norm_res_kernel, tl=tl, nk=nk, per_row=per_row),
        grid=(nb, L // tl, nk),
        in_specs=[
            pl.BlockSpec((1, tl, tk), lambda b, i, k: (b, i, k)),
            pl.BlockSpec((None, tk, D), lambda b, i, k: (layer, k, 0)),
            pl.BlockSpec((1, tl, D), lambda b, i, k: (b, i, 0)),
            pl.BlockSpec((1, r, D), lambda b, i, k: (b, i if per_row else 0, 0)),
            pl.BlockSpec((1, 1, D), lambda b, i, k: (0, 0, 0)),
        ],
        out_specs=pl.BlockSpec((1, tl, D), lambda b, i, k: (b, i, 0)),
        out_shape=jax.ShapeDtypeStruct((nb, L, D), F32),
        compiler_params=_params(("parallel", "parallel", "arbitrary")),
        name="matmul_norm_residual",
    )(a, w, x, gate, g.reshape(1, 1, D))


def _scan_rows(x, op, row):
    d = 1
    while d < x.shape[0]:
        shifted = pltpu.roll(x, d, axis=0)
        x = jnp.where(row >= d, op(x, shifted), x)
        d *= 2
    return x


def _mlstm_kernel(qk_ref, v_ref, og_ref, gt_ref, bias_ref, gh_ref, c0_ref, n0_ref, m0_ref,
                  o_ref, c_out_ref, n_out_ref, m_out_ref, c_scr, n_scr, m_scr,
                  *, valid_len, nc, dqk, dv):
    c = pl.program_id(1)
    L = MLSTM_CHUNK

    @pl.when(c == 0)
    def _():
        c_scr[...] = c0_ref[0]
        n_scr[...] = n0_ref[0]
        m_scr[...] = m0_ref[0]

    gates = gt_ref[0] + bias_ref[...]
    li = gates[:, :LANES]
    gf = gates[:, LANES:]
    lf = jnp.minimum(gf, 0.0) - jnp.log1p(jnp.exp(-jnp.abs(gf)))
    row = lax.broadcasted_iota(jnp.int32, (L, LANES), 0)
    col = lax.broadcasted_iota(jnp.int32, (L, LANES), 1)
    if valid_len < L:
        li = jnp.where(row < valid_len, li, NEG_BIG)
        lf = jnp.where(row < valid_len, lf, 0.0)

    b = _scan_rows(lf, jnp.add, row)
    u = li - b
    m_prev = m_scr[...]
    m_t = b + jnp.maximum(m_prev, _scan_rows(u, jnp.maximum, row))
    d_inter = jnp.exp(b + m_prev - m_t)
    ct = b - m_t
    inv_floor = jnp.exp(-m_t)
    b_end = b[L - 1:L, :]
    m_end = m_t[L - 1:L, :]
    w_all = jnp.exp(b_end - b + li - m_end)
    decay = jnp.exp(b_end + m_prev - m_end)
    u_t = u.T
    causal = row >= col

    for h in range(MH):
        q = qk_ref[0, :, h * dqk:(h + 1) * dqk]
        k = qk_ref[0, :, MH * dqk + h * dqk:MH * dqk + (h + 1) * dqk]
        v = v_ref[0, :, h * dv:(h + 1) * dv]
        log_d = jnp.where(causal, ct[:, h:h + 1] + u_t[h:h + 1, :], NEG_BIG)
        s = _dot_nt(q, k) * jnp.exp(log_d)
        di = d_inter[:, h:h + 1]
        c_prev = c_scr[h]
        num = _dot(s.astype(BF16), v) + di * _dot(q, c_prev.astype(BF16))
        n_prev = n_scr[h:h + 1, :]
        qn_state = jnp.sum(q.astype(F32) * n_prev.astype(BF16).astype(F32), axis=-1, keepdims=True)
        qn = jnp.sum(s, axis=-1, keepdims=True) + di * qn_state
        hh = num / jnp.maximum(jnp.abs(qn), inv_floor[:, h:h + 1])
        hh = hh * lax.rsqrt(jnp.mean(hh * hh, axis=-1, keepdims=True) + EPS)
        hh = hh * gh_ref[:, h * dv:(h + 1) * dv]
        og = og_ref[0, :, h * dv:(h + 1) * dv]
        o_ref[0, :, h * dv:(h + 1) * dv] = (hh * jax.nn.sigmoid(og)).astype(BF16)

        kw = k.astype(F32) * w_all[:, h:h + 1]
        dec = decay[:, h:h + 1]
        c_scr[h] = dec * c_prev + _dot_tn(kw.astype(BF16), v)
        n_scr[h:h + 1, :] = dec * n_prev + jnp.sum(kw, axis=0, keepdims=True)

    m_scr[...] = m_end

    @pl.when(c == nc - 1)
    def _():
        c_out_ref[0] = c_scr[...]
        n_out_ref[0] = n_scr[...]
        m_out_ref[0] = m_scr[...]


def _mlstm(qk, v, og, gates, bias_row, g_head, c0, n0, m0, valid_len):
    nb, L, vt = v.shape
    dv = vt // MH
    dqk = qk.shape[2] // (2 * MH)
    nc = L // MLSTM_CHUNK
    blk = lambda width: pl.BlockSpec((1, MLSTM_CHUNK, width), lambda b, c: (b, c, 0))
    return pl.pallas_call(
        functools.partial(_mlstm_kernel, valid_len=valid_len, nc=nc, dqk=dqk, dv=dv),
        grid=(nb, nc),
        in_specs=[
            blk(qk.shape[2]), blk(vt), blk(vt), blk(2 * LANES),
            pl.BlockSpec((1, 2 * LANES), lambda b, c: (0, 0)),
            pl.BlockSpec((1, vt), lambda b, c: (0, 0)),
            pl.BlockSpec((1, MH, dqk, dv), lambda b, c: (b, 0, 0, 0)),
            pl.BlockSpec((1, MH, dqk), lambda b, c: (b, 0, 0)),
            pl.BlockSpec((1, 1, LANES), lambda b, c: (b, 0, 0)),
        ],
        out_specs=[
            blk(vt),
            pl.BlockSpec((1, MH, dqk, dv), lambda b, c: (b, 0, 0, 0)),
            pl.BlockSpec((1, MH, dqk), lambda b, c: (b, 0, 0)),
            pl.BlockSpec((1, 1, LANES), lambda b, c: (b, 0, 0)),
        ],
        out_shape=[
            jax.ShapeDtypeStruct((nb, L, vt), BF16),
            jax.ShapeDtypeStruct((nb, MH, dqk, dv), F32),
            jax.ShapeDtypeStruct((nb, MH, dqk), F32),
            jax.ShapeDtypeStruct((nb, 1, LANES), F32),
        ],
        scratch_shapes=[
            pltpu.VMEM((MH, dqk, dv), F32),
            pltpu.VMEM((MH, dqk), F32),
            pltpu.VMEM((1, LANES), F32),
        ],
        compiler_params=_params(("parallel", "arbitrary")),
        name="mlstm_chunks",
    )(qk, v, og, gates, bias_row, g_head.reshape(1, vt), c0, n0, m0)


def _suffix_matrix():
    j = lax.broadcasted_iota(jnp.int32, (2 * SB_BLOCK, 2 * SB_BLOCK), 0) % SB_BLOCK
    s = lax.broadcasted_iota(jnp.int32, (2 * SB_BLOCK, 2 * SB_BLOCK), 1)
    return jnp.where((s >= SB_BLOCK) | (j > s), 1.0, 0.0).astype(BF16)


def _sb_scores(z, mask, suffix_mat):
    soft = jnp.log(1.0 + jnp.exp(-jnp.abs(z)))
    log_beta = jnp.minimum(z, 0.0) - soft
    log_1mb = log_beta - z
    if mask is not None:
        log_1mb = jnp.where(mask, log_1mb, 0.0)
    hi = log_1mb.astype(BF16)
    lo = (log_1mb - hi.astype(F32)).astype(BF16)
    sums = _dot(jnp.concatenate([hi, lo], axis=1), suffix_mat)
    return log_beta, sums[:, :SB_BLOCK], sums[:, SB_BLOCK:]


def _sb_visit(zs, masks, value_fns, carry, suffix_mat):
    scores = [_sb_scores(z, m, suffix_mat) for z, m in zip(zs, masks)]
    pv = None
    for (log_beta, later, total), mask, value_fn in zip(scores, masks, value_fns):
        a = jnp.exp(log_beta + later + carry)
        if mask is not None:
            a = jnp.where(mask, a, 0.0)
        out = value_fn(a.astype(BF16))
        pv = out if pv is None else pv + out
        carry = carry + total
    return pv, carry


def _sb_prompt_kernel(q_ref, k_ref, v_ref, bias_ref, o_ref, k_scr, v_scr, acc_scr, carry_scr,
                      *, n_qt, scale):
    k_scr[...] = k_ref[0].astype(BF16)
    v_scr[...] = v_ref[0].astype(BF16)
    suffix_mat = _suffix_matrix()
    bias = bias_ref[...]
    sub = SB_QTILE // SB_BLOCK

    def visit(q, blocks, r0, mask):
        zs, value_fns = [], []
        for j in blocks:
            keys = pl.ds(pl.multiple_of(j * SB_BLOCK, SB_BLOCK), SB_BLOCK)
            zs.append(_dot_nt(q, k_scr[keys, :]) * scale + bias)
            value_fns.append(lambda a, keys=keys: _dot(a, v_scr[keys, :]))
        pv, carry = _sb_visit(zs, [mask] * len(zs), value_fns, carry_scr[r0:, :], suffix_mat)
        acc_scr[r0:, :] += pv
        carry_scr[r0:, :] = carry

    for t in range(n_qt):
        q = q_ref[0, t * SB_QTILE:(t + 1) * SB_QTILE, :]
        acc_scr[...] = jnp.zeros_like(acc_scr)
        carry_scr[...] = jnp.zeros_like(carry_scr)
        for c in reversed(range(sub)):
            r0 = c * SB_BLOCK
            row = lax.broadcasted_iota(jnp.int32, (SB_QTILE - r0, SB_BLOCK), 0)
            col = lax.broadcasted_iota(jnp.int32, (SB_QTILE - r0, SB_BLOCK), 1)
            visit(q[r0:, :], [t * sub + c], r0, col < row)

        def older(i, _, t=t, q=q):
            newest = (t - i) * sub - 1
            visit(q, [newest - d for d in range(sub)], 0, None)
            return 0

        lax.fori_loop(0, t, older, 0)
        o_ref[0, t * SB_QTILE:(t + 1) * SB_QTILE, :] = acc_scr[...].astype(BF16)


def _sb_prompt(q, k, v, bias_lanes):
    nb, L, hd = q.shape
    dh = hd // SB_HEADS
    spec = pl.BlockSpec((1, L, dh), lambda b, h: (b, 0, h))
    return pl.pallas_call(
        functools.partial(_sb_prompt_kernel, n_qt=L // SB_QTILE, scale=dh ** -0.5),
        grid=(nb, SB_HEADS),
        in_specs=[spec, spec, spec, pl.BlockSpec((1, dh), lambda b, h: (0, h))],
        out_specs=spec,
        out_shape=jax.ShapeDtypeStruct((nb, L, hd), BF16),
        scratch_shapes=[pltpu.VMEM((L, dh), BF16), pltpu.VMEM((L, dh), BF16),
                        pltpu.VMEM((SB_QTILE, dh), F32), pltpu.VMEM((SB_QTILE, SB_BLOCK), F32)],
        compiler_params=_params(("parallel", "parallel")),
        name="stickbreak_prompt",
    )(q, k, v, bias_lanes)


def _sb_decode_kernel(pt_ref, q_ref, bias_ref, kn_ref, vn_ref, ka_ref, va_ref, kb_ref, vb_ref, o_ref,
                      carry_scr, acc_scr, qbd_scr, expand_scr, hmask_scr, *, n_steps, n_new, scale):
    p = pl.program_id(1)
    rows = SB_HEADS * n_new
    dh = q_ref.shape[2] // SB_HEADS
    suffix_mat = _suffix_matrix()

    @pl.when(p == 0)
    def _():
        carry_scr[...] = jnp.zeros_like(carry_scr)
        acc_scr[...] = jnp.zeros_like(acc_scr)
        q_rep = jnp.concatenate([q_ref[0]] * SB_HEADS, axis=0)
        lane_head = lax.broadcasted_iota(jnp.int32, q_rep.shape, 1) // dh
        row_head = lax.broadcasted_iota(jnp.int32, q_rep.shape, 0) // n_new
        qbd_scr[...] = jnp.where(lane_head == row_head, q_rep, 0.0).astype(BF16)
        tok = lax.broadcasted_iota(jnp.int32, expand_scr.shape, 0)
        c = lax.broadcasted_iota(jnp.int32, expand_scr.shape, 1)
        expand_scr[...] = jnp.where(c // SB_HEADS == tok, 1.0, 0.0).astype(BF16)
        r = lax.broadcasted_iota(jnp.int32, hmask_scr.shape, 0)
        c = lax.broadcasted_iota(jnp.int32, hmask_scr.shape, 1)
        hmask_scr[...] = jnp.where(c % SB_HEADS == r // n_new, 1.0, 0.0).astype(BF16)

    def attend(kv_refs, mask):
        zs, value_fns = [], []
        for k_ref, v_ref in kv_refs:
            k_all = jnp.concatenate(
                [k_ref[pl.ds(h, PAGE_SIZE, stride=SB_HEADS), :].astype(BF16) for h in range(SB_HEADS)],
                axis=1)
            zs.append(_dot_nt(qbd_scr[...], k_all) * scale + bias_ref[...])

            def values(a, v_ref=v_ref):
                spread = _dot(a, expand_scr[...]).astype(BF16) * hmask_scr[...]
                return _dot(spread, v_ref[...].astype(BF16))

            value_fns.append(values)
        pv, carry = _sb_visit(zs, [mask] * len(zs), value_fns, carry_scr[...], suffix_mat)
        acc_scr[...] += pv
        carry_scr[...] = carry

    @pl.when(p == 0)
    def _():
        row = lax.broadcasted_iota(jnp.int32, (rows, SB_BLOCK), 0)
        col = lax.broadcasted_iota(jnp.int32, (rows, SB_BLOCK), 1)
        attend([(kn_ref, vn_ref)], col < row % n_new)

    @pl.when(p > 0)
    def _():
        attend([(ka_ref, va_ref), (kb_ref, vb_ref)], None)

    @pl.when(p == n_steps - 1)
    def _():
        o_ref[0] = acc_scr[...]


def _sb_decode(page_table, q, bias_rows, k_new, v_new, pool_k, pool_v, layer):
    nb, n_new, hd = q.shape
    dh = hd // SB_HEADS
    n_pages = page_table.shape[1]
    rows = SB_HEADS * n_new
    prow = PAGE_SIZE * SB_HEADS
    assert n_pages % 2 == 0 and rows == SB_BLOCK and PAGE_SIZE == SB_BLOCK
    n_steps = n_pages // 2 + 1
    n_layers, n_pool = pool_v.shape[:2]
    pool_k = pool_k.reshape(n_layers, n_pool, prow, dh)
    pool_v = pool_v.reshape(n_layers, n_pool, prow, dh)
    k_new = k_new.reshape(nb, prow, dh)
    v_new = v_new.reshape(nb, prow, dh)

    def page_spec(slot):
        return pl.BlockSpec(
            (None, None, prow, dh),
            lambda b, p, pt: (layer, pt[b, n_pages - 2 * jnp.maximum(p, 1) + 1 - slot], 0, 0))

    new_spec = pl.BlockSpec((None, prow, dh), lambda b, p, pt: (b, 0, 0))
    grid_spec = pltpu.PrefetchScalarGridSpec(
        num_scalar_prefetch=1,
        grid=(nb, n_steps),
        in_specs=[
            pl.BlockSpec((1, n_new, hd), lambda b, p, pt: (b, 0, 0)),
            pl.BlockSpec((rows, SB_BLOCK), lambda b, p, pt: (0, 0)),
            new_spec, new_spec,
            page_spec(0), page_spec(0), page_spec(1), page_spec(1),
        ],
        out_specs=pl.BlockSpec((1, rows, dh), lambda b, p, pt: (b, 0, 0)),
        scratch_shapes=[pltpu.VMEM((rows, SB_BLOCK), F32), pltpu.VMEM((rows, dh), F32),
                        pltpu.VMEM((rows, hd), BF16), pltpu.VMEM((PAGE_SIZE, prow), BF16),
                        pltpu.VMEM((rows, prow), BF16)],
    )
    return pl.pallas_call(
        functools.partial(_sb_decode_kernel, n_steps=n_steps, n_new=n_new, scale=dh ** -0.5),
        grid_spec=grid_spec,
        out_shape=jax.ShapeDtypeStruct((nb, rows, dh), F32),
        compiler_params=_params(("parallel", "arbitrary")),
        name="stickbreak_decode",
    )(page_table, q, bias_rows, k_new, v_new, pool_k, pool_v, pool_k, pool_v)


def _ffn(x, mods, g_pre, g_post, w_in, w_out, layer, tf=512):
    d_ff = w_out.shape[1]
    (act,) = _modmm(x, g_pre, mods[3], mods[4], w_in, layer, (0, d_ff), tf, d_ff // tf,
                    (BF16,), _swiglu_epilogue)
    return _mm_norm_res(act, w_out, layer, x, mods[5], g_post)


def _mlstm_layer(x, mods, g_norm, w_in, w_out, layer, b_gate, g_head, c0, n0, m0, chunk_rows):
    D = x.shape[2]
    qk_tot = (w_in.shape[2] - 2 * MH) // 3
    dqk = qk_tot // (2 * MH)
    tn = 256
    w_gate = jnp.zeros((D, 2 * LANES), F32)
    w_gate = w_gate.at[:, :MH].set(w_in[layer, :, 3 * qk_tot:3 * qk_tot + MH])
    w_gate = w_gate.at[:, LANES:LANES + MH].set(w_in[layer, :, 3 * qk_tot + MH:])
    bias_row = jnp.zeros((1, 2 * LANES), F32)
    bias_row = bias_row.at[0, :MH].set(b_gate[0].astype(F32))
    bias_row = bias_row.at[0, LANES:LANES + MH].set(b_gate[1].astype(F32))
    epilogue = functools.partial(_mlstm_proj_epilogue, q_tiles=(MH * dqk) // tn, q_scale=dqk ** -0.5)
    qk, v, og, gates = _modmm(x, g_norm[0], mods[0], mods[1], w_in, layer, (0, qk_tot, 2 * qk_tot), tn,
                              qk_tot // tn, (BF16, BF16, F32), epilogue, w_gate=w_gate)
    m0p = jnp.zeros((m0.shape[0], 1, LANES), F32).at[:, 0, :MH].set(m0.astype(F32))
    if chunk_rows == MLSTM_CHUNK:
        h, c_new, n_new, m_new = _mlstm(qk, v, og, gates, bias_row, g_head,
                                        c0.astype(F32), n0.astype(F32), m0p, MLSTM_CHUNK)
    else:
        nb = x.shape[1] // chunk_rows

        def pad(a):
            a = a.reshape(nb, chunk_rows, a.shape[2])
            return jnp.pad(a, ((0, 0), (0, MLSTM_CHUNK - chunk_rows), (0, 0)))

        h, c_new, n_new, m_new = _mlstm(pad(qk), pad(v), pad(og), pad(gates), bias_row, g_head,
                                        c0.astype(F32), n0.astype(F32), m0p, chunk_rows)
        h = h[:, :chunk_rows].reshape(1, nb * chunk_rows, h.shape[2])
    x = _mm_norm_res(h, w_out, layer, x, mods[2], g_norm[1])
    return x, c_new, n_new, m_new[:, 0, :MH]


def kernel(x_prompt, x_sample, state_C, state_n, state_m, cache_k, cache_v, page_table, c_prompt, c_sample, w_ada, b_ada, g_norm, w_in_a, b_gate_a, g_head_a, w_out_a, w_in_b, b_sb, w_out_b, w_ffn_in, w_ffn_out):
    bp, seq, D = x_prompt.shape
    bs, dec_seq, _ = x_sample.shape
    depth = w_ada.shape[0]
    dh = D // SB_HEADS

    rows = -(-(bp + bs) // 8) * 8
    c_all = jnp.zeros((rows, D), F32).at[:bp].set(c_prompt).at[bp:bp + bs].set(c_sample)
    ada = _ada(c_all, w_ada, b_ada)

    xp = x_prompt
    xs = x_sample.reshape(1, bs * dec_seq, D)
    outs = {name: [] for name in ("kp", "vp", "ks", "vs", "Cp", "np", "mp", "Cs", "ns", "ms")}
    for i in range(depth):
        j = i // 2
        mods_p = [ada[i, :bp, s * D:(s + 1) * D].reshape(bp, 1, D) for s in range(6)]
        mods_s = [jnp.repeat(ada[i, bp:bp + bs, s * D:(s + 1) * D], dec_seq, axis=0).reshape(1, bs * dec_seq, D)
                  for s in range(6)]
        if i % 2 == 0:
            dqk = (w_in_a.shape[2] - 2 * MH) // 3 // (2 * MH)
            dv = 2 * dqk
            zc = jnp.zeros((bp, MH, dqk, dv), F32)
            zn = jnp.zeros((bp, MH, dqk), F32)
            zm = jnp.zeros((bp, MH), F32)
            xp, Cp, n_p, m_p = _mlstm_layer(xp, mods_p, g_norm[i], w_in_a, w_out_a, j, b_gate_a[j], g_head_a[j],
                                            zc, zn, zm, MLSTM_CHUNK)
            xs, Cs, n_s, m_s = _mlstm_layer(xs, mods_s, g_norm[i], w_in_a, w_out_a, j, b_gate_a[j], g_head_a[j],
                                            state_C[j], state_n[j], state_m[j], dec_seq)
            outs["Cp"].append(Cp); outs["np"].append(n_p); outs["mp"].append(m_p)
            outs["Cs"].append(Cs); outs["ns"].append(n_s); outs["ms"].append(m_s)
        else:
            tn = 256
            bias = b_sb[j].astype(F32)
            qp, kp, vp = _modmm(xp, g_norm[i, 0], mods_p[0], mods_p[1], w_in_b, j, (0, D, 2 * D), tn,
                                D // tn, (BF16, F32, F32), _sb_proj_epilogue)
            op = _sb_prompt(qp, kp, vp, jnp.repeat(bias, dh).reshape(1, D))
            xp = _mm_norm_res(op, w_out_b, j, xp, mods_p[2], g_norm[i, 1])
            qs, kn, vn = _modmm(xs, g_norm[i, 0], mods_s[0], mods_s[1], w_in_b, j, (0, D, 2 * D), tn,
                                D // tn, (F32, F32, F32), _plain_epilogue)

            def as_page(a):
                a = a.reshape(bs, dec_seq, SB_HEADS, dh)
                return jnp.pad(a, ((0, 0), (0, PAGE_SIZE - dec_seq), (0, 0), (0, 0)))

            os_ = _sb_decode(page_table, qs.reshape(bs, dec_seq, D),
                             jnp.broadcast_to(jnp.repeat(bias, dec_seq)[:, None], (SB_HEADS * dec_seq, SB_BLOCK)),
                             as_page(kn), as_page(vn), cache_k, cache_v, j)
            os_ = os_.reshape(bs, SB_HEADS, dec_seq, dh).transpose(0, 2, 1, 3).reshape(1, bs * dec_seq, D)
            xs = _mm_norm_res(os_.astype(BF16), w_out_b, j, xs, mods_s[2], g_norm[i, 1])
            outs["kp"].append(kp.reshape(bp, seq, SB_HEADS, dh)); outs["vp"].append(vp.reshape(bp, seq, SB_HEADS, dh))
            outs["ks"].append(kn.reshape(bs, dec_seq, SB_HEADS, dh)); outs["vs"].append(vn.reshape(bs, dec_seq, SB_HEADS, dh))
        xp = _ffn(xp, mods_p, g_norm[i, 2], g_norm[i, 3], w_ffn_in, w_ffn_out, i)
        xs = _ffn(xs, mods_s, g_norm[i, 2], g_norm[i, 3], w_ffn_in, w_ffn_out, i)

    st = lambda name: jnp.stack(outs[name])
    return (xp, xs.reshape(bs, dec_seq, D), st("kp"), st("vp"), st("ks"), st("vs"),
            st("Cp"), st("np"), st("mp"), st("Cs"), st("ns"), st("ms"))
```

```python
import functools

import jax
import jax.numpy as jnp
from jax import lax
from jax.experimental import pallas as pl
from jax.experimental.pallas import tpu as pltpu

F32 = jnp.float32
BF16 = jnp.bfloat16

EPS = 1e-6
MH = 8
MLSTM_CHUNK = 128
SB_HEADS = 16
SB_BLOCK = 128
SB_QTILE = 512
PAGE_SIZE = 128
NEG_BIG = -1e30

LANES = 128
SUBLANES = 8
VMEM_LIMIT = 52 * 1024 * 1024
ACC_COLS = 512
PROJ_TILE = 512


def _dot(a, b):
    return jnp.dot(a, b, preferred_element_type=F32)


def _dot_nt(a, b):
    return lax.dot_general(a, b, (((1,), (1,)), ((), ())), preferred_element_type=F32)


def _dot_tn(a, b):
    return lax.dot_general(a, b, (((0,), (0,)), ((), ())), preferred_element_type=F32)


def _params(sem):
    return pltpu.CompilerParams(dimension_semantics=sem, vmem_limit_bytes=VMEM_LIMIT)


def _ada_kernel(c_ref, w_ref, b_ref, o_ref):
    c = c_ref[...]
    a = (c * jax.nn.sigmoid(c)).astype(BF16)
    o_ref[0] = _dot(a, w_ref[0].astype(BF16)) + b_ref[0]


def _ada(c_all, w_ada, b_ada, tn=1024):
    depth, d, n = w_ada.shape
    rows = c_all.shape[0]
    return pl.pallas_call(
        _ada_kernel,
        grid=(depth, n // tn),
        in_specs=[
            pl.BlockSpec((rows, d), lambda l, j: (0, 0)),
            pl.BlockSpec((1, d, tn), lambda l, j: (l, 0, j)),
            pl.BlockSpec((1, 1, tn), lambda l, j: (l, 0, j)),
        ],
        out_specs=pl.BlockSpec((1, rows, tn), lambda l, j: (l, 0, j)),
        out_shape=jax.ShapeDtypeStruct((depth, rows, n), F32),
        compiler_params=_params(("parallel", "parallel")),
        name="ada_params",
    )(c_all, w_ada, b_ada.reshape(depth, 1, n))


def _row_chunks(tl):
    rc = min(tl, 256)
    return rc, tl // rc


def _modulated(x, g, scale, shift):
    r = lax.rsqrt(jnp.mean(x * x, axis=-1, keepdims=True) + EPS)
    return ((x * r) * g) * (1.0 + scale) + shift


def _modmm_kernel(*refs, n_slices, n_tiles, epilogue, sequential, has_gate, tl, per_row):
    x_ref, g_ref, sh_ref, sc_ref = refs[:4]
    n_w = 1 if sequential else n_slices
    w_refs = refs[4:4 + n_w]
    pos = 4 + n_w
    if has_gate:
        wg_ref = refs[pos]
        pos += 1
    n_out = len(refs) - pos - 1 - (1 if has_gate else 0)
    out_refs = refs[pos:pos + n_out]
    pos += n_out
    if has_gate:
        go_ref = refs[pos]
        pos += 1
    h_scr = refs[pos]
    j = pl.program_id(2)

    @pl.when(j == 0)
    def _():
        rc, n_chunks = _row_chunks(tl)

        def chunk(c):
            rows = pl.ds(pl.multiple_of(c * rc, rc), rc)
            x = x_ref[0, rows, :]
            if per_row:
                sc, sh = sc_ref[0, rows, :], sh_ref[0, rows, :]
            else:
                sc, sh = sc_ref[0], sh_ref[0]
            h_scr[rows, :] = _modulated(x, g_ref[0], sc, sh).astype(BF16)

        if n_chunks == 1:
            chunk(0)
        else:
            pl.loop(0, n_chunks)(chunk)
        if has_gate:
            go_ref[0] = _dot(h_scr[...], wg_ref[...].astype(BF16))

    if sequential:
        for s, out_ref in enumerate(out_refs):
            @pl.when(j // n_tiles == s)
            def _(s=s, out_ref=out_ref):
                y = _dot(h_scr[...], w_refs[0][...].astype(BF16))
                out_ref[0] = epilogue[s](y, j - s * n_tiles)
    else:
        hb = h_scr[...]
        ys = [_dot(hb, w_ref[...].astype(BF16)) for w_ref in w_refs]
        epilogue(ys, out_refs, j)


def _modmm(x, g, shift, scale, w, layer, slice_starts, tn, n_tiles, out_dtypes, epilogue,
           w_gate=None, tl=1024):
    nb, L, D = x.shape
    tl = min(tl, L)
    per_row = shift.shape[1] != 1
    r = tl if per_row else 1
    n_slices = len(slice_starts)
    has_gate = w_gate is not None
    sequential = isinstance(epilogue, (list, tuple))
    if sequential:
        assert all(s0 == slice_starts[0] + s * n_tiles * tn for s, s0 in enumerate(slice_starts))

    def mod_map(b, i, j):
        return (b, i if per_row else 0, 0)

    in_specs = [
        pl.BlockSpec((1, tl, D), lambda b, i, j: (b, i, 0)),
        pl.BlockSpec((1, 1, D), lambda b, i, j: (0, 0, 0)),
        pl.BlockSpec((1, r, D), mod_map),
        pl.BlockSpec((1, r, D), mod_map),
    ]
    args = [x, g.reshape(1, 1, D), shift, scale]
    for s0 in slice_starts[:1] if sequential else slice_starts:
        off = s0 // tn
        in_specs.append(pl.BlockSpec((None, D, tn), lambda b, i, j, off=off: (layer, 0, off + j)))
        args.append(w)
    if has_gate:
        gw = w_gate.shape[1]
        in_specs.append(pl.BlockSpec((D, gw), lambda b, i, j: (0, 0)))
        args.append(w_gate)
    if sequential:
        out_specs = [
            pl.BlockSpec((1, tl, tn), lambda b, i, j, s=s: (b, i, jnp.clip(j - s * n_tiles, 0, n_tiles - 1)))
            for s in range(n_slices)]
    else:
        out_specs = [pl.BlockSpec((1, tl, tn), lambda b, i, j: (b, i, j)) for _ in out_dtypes]
    out_shape = [jax.ShapeDtypeStruct((nb, L, n_tiles * tn), dt) for dt in out_dtypes]
    if has_gate:
        out_specs.append(pl.BlockSpec((1, tl, gw), lambda b, i, j: (b, i, 0)))
        out_shape.append(jax.ShapeDtypeStruct((nb, L, gw), F32))
    return pl.pallas_call(
        functools.partial(_modmm_kernel, n_slices=n_slices, n_tiles=n_tiles, epilogue=epilogue,
                          sequential=sequential, has_gate=has_gate, tl=tl, per_row=per_row),
        grid=(nb, L // tl, n_tiles * n_slices if sequential else n_tiles),
        in_specs=in_specs,
        out_specs=out_specs,
        out_shape=out_shape,
        scratch_shapes=[pltpu.VMEM((tl, D), BF16)],
        compiler_params=_params(("parallel", "parallel", "arbitrary")),
        name="modulated_matmul",
    )(*args)


def _as_bf16(y, tile):
    return y.astype(BF16)


def _as_f32(y, tile):
    return y


def _scaled_q_then_k(y, tile, *, q_tiles, q_scale):
    return (y * jnp.where(tile < q_tiles, q_scale, 1.0).astype(F32)).astype(BF16)


def _swiglu_epilogue(ys, out_refs, j):
    gate, up = ys
    out_refs[0][0] = ((gate * jax.nn.sigmoid(gate)) * up).astype(BF16)


def _mm_norm_res_kernel(a_ref, w_ref, x_ref, gate_ref, g_ref, o_ref, *, tl, nk, per_row):
    k = pl.program_id(2)

    def accumulate(first):
        a = a_ref[0]
        cw = min(ACC_COLS, o_ref.shape[2])
        for c in range(o_ref.shape[2] // cw):
            cols = slice(c * cw, (c + 1) * cw)
            y = _dot(a, w_ref[:, cols].astype(BF16))
            if first:
                o_ref[0, :, cols] = y
            else:
                o_ref[0, :, cols] += y

    pl.when(k == 0)(functools.partial(accumulate, True))
    pl.when(k > 0)(functools.partial(accumulate, False))

    @pl.when(k == nk - 1)
    def _():
        rc, n_chunks = _row_chunks(tl)

        def chunk(c):
            rows = pl.ds(pl.multiple_of(c * rc, rc), rc)
            acc = o_ref[0, rows, :]
            r = lax.rsqrt(jnp.mean(acc * acc, axis=-1, keepdims=True) + EPS)
            gate = gate_ref[0, rows, :] if per_row else gate_ref[0]
            o_ref[0, rows, :] = x_ref[0, rows, :] + gate * ((acc * r) * g_ref[0])

        if n_chunks == 1:
            chunk(0)
        else:
            pl.loop(0, n_chunks)(chunk)


def _mm_norm_res(a, w, layer, x, gate, g, tk=512, tl=1024):
    nb, L, K = a.shape
    D = w.shape[2]
    tl = min(tl, L)
    per_row = gate.shape[1] != 1
    r = tl if per_row else 1
    nk = K // tk
    return pl.pallas_call(
        functools.partial(_mm_norm_res_kernel, tl=tl, nk=nk, per_row=per_row),
        grid=(nb, L // tl, nk),
        in_specs=[
            pl.BlockSpec((1, tl, tk), lambda b, i, k: (b, i, k)),
            pl.BlockSpec((None, tk, D), lambda b, i, k: (layer, k, 0)),
            pl.BlockSpec((1, tl, D), lambda b, i, k: (b, i, 0)),
            pl.BlockSpec((1, r, D), lambda b, i, k: (b, i if per_row else 0, 0)),
            pl.BlockSpec((1, 1, D), lambda b, i, k: (0, 0, 0)),
        ],
        out_specs=pl.BlockSpec((1, tl, D), lambda b, i, k: (b, i, 0)),
        out_shape=jax.ShapeDtypeStruct((nb, L, D), F32),
        compiler_params=_params(("parallel", "parallel", "arbitrary")),
        name="matmul_norm_residual",
    )(a, w, x, gate, g.reshape(1, 1, D))


def _scan_rows(x, op, row):
    d = 1
    while d < x.shape[0]:
        shifted = pltpu.roll(x, d, axis=0)
        x = jnp.where(row >= d, op(x, shifted), x)
        d *= 2
    return x


def _mlstm_kernel(qk_ref, v_ref, og_ref, gt_ref, bias_ref, gh_ref, c0_ref, n0_ref, m0_ref,
                  o_ref, c_out_ref, n_out_ref, m_out_ref, c_scr, n_scr, m_scr,
                  *, valid_len, nc, dqk, dv):
    c = pl.program_id(1)
    L = MLSTM_CHUNK

    @pl.when(c == 0)
    def _():
        c_scr[...] = c0_ref[0]
        n_scr[...] = n0_ref[0]
        m_scr[...] = m0_ref[0]

    gates = gt_ref[0] + bias_ref[...]
    li = gates[:, :LANES]
    gf = gates[:, LANES:]
    lf = jnp.minimum(gf, 0.0) - jnp.log1p(jnp.exp(-jnp.abs(gf)))
    row = lax.broadcasted_iota(jnp.int32, (L, LANES), 0)
    col = lax.broadcasted_iota(jnp.int32, (L, LANES), 1)
    if valid_len < L:
        li = jnp.where(row < valid_len, li, NEG_BIG)
        lf = jnp.where(row < valid_len, lf, 0.0)

    b = _scan_rows(lf, jnp.add, row)
    u = li - b
    m_prev = m_scr[...]
    m_t = b + jnp.maximum(m_prev, _scan_rows(u, jnp.maximum, row))
    d_inter = jnp.exp(b + m_prev - m_t)
    ct = b - m_t
    inv_floor = jnp.exp(-m_t)
    b_end = b[L - 1:L, :]
    m_end = m_t[L - 1:L, :]
    w_all = jnp.exp(b_end - b + li - m_end)
    decay = jnp.exp(b_end + m_prev - m_end)
    u_t = u.T
    causal = row >= col

    for h in range(MH):
        q = qk_ref[0, :, h * dqk:(h + 1) * dqk]
        k = qk_ref[0, :, MH * dqk + h * dqk:MH * dqk + (h + 1) * dqk]
        v = v_ref[0, :, h * dv:(h + 1) * dv]
        log_d = jnp.where(causal, ct[:, h:h + 1] + u_t[h:h + 1, :], NEG_BIG)
        s = _dot_nt(q, k) * jnp.exp(log_d)
        di = d_inter[:, h:h + 1]
        c_prev = c_scr[h]
        num = _dot(s.astype(BF16), v) + di * _dot(q, c_prev.astype(BF16))
        n_prev = n_scr[h:h + 1, :]
        qn_state = jnp.sum(q.astype(F32) * n_prev.astype(BF16).astype(F32), axis=-1, keepdims=True)
        qn = jnp.sum(s, axis=-1, keepdims=True) + di * qn_state
        hh = num / jnp.maximum(jnp.abs(qn), inv_floor[:, h:h + 1])
        hh = hh * lax.rsqrt(jnp.mean(hh * hh, axis=-1, keepdims=True) + EPS)
        hh = hh * gh_ref[:, h * dv:(h + 1) * dv]
        og = og_ref[0, :, h * dv:(h + 1) * dv]
        o_ref[0, :, h * dv:(h + 1) * dv] = (hh * jax.nn.sigmoid(og)).astype(BF16)

        kw = k.astype(F32) * w_all[:, h:h + 1]
        dec = decay[:, h:h + 1]
        c_scr[h] = dec * c_prev + _dot_tn(kw.astype(BF16), v)
        n_scr[h:h + 1, :] = dec * n_prev + jnp.sum(kw, axis=0, keepdims=True)

    m_scr[...] = m_end

    @pl.when(c == nc - 1)
    def _():
        c_out_ref[0] = c_scr[...]
        n_out_ref[0] = n_scr[...]
        m_out_ref[0] = m_scr[...]


def _mlstm(qk, v, og, gates, bias_row, g_head, c0, n0, m0, valid_len):
    nb, L, vt = v.shape
    dv = vt // MH
    dqk = qk.shape[2] // (2 * MH)
    nc = L // MLSTM_CHUNK
    blk = lambda width: pl.BlockSpec((1, MLSTM_CHUNK, width), lambda b, c: (b, c, 0))
    return pl.pallas_call(
        functools.partial(_mlstm_kernel, valid_len=valid_len, nc=nc, dqk=dqk, dv=dv),
        grid=(nb, nc),
        in_specs=[
            blk(qk.shape[2]), blk(vt), blk(vt), blk(2 * LANES),
            pl.BlockSpec((1, 2 * LANES), lambda b, c: (0, 0)),
            pl.BlockSpec((1, vt), lambda b, c: (0, 0)),
            pl.BlockSpec((1, MH, dqk, dv), lambda b, c: (b, 0, 0, 0)),
            pl.BlockSpec((1, MH, dqk), lambda b, c: (b, 0, 0)),
            pl.BlockSpec((1, 1, LANES), lambda b, c: (b, 0, 0)),
        ],
        out_specs=[
            blk(vt),
            pl.BlockSpec((1, MH, dqk, dv), lambda b, c: (b, 0, 0, 0)),
            pl.BlockSpec((1, MH, dqk), lambda b, c: (b, 0, 0)),
            pl.BlockSpec((1, 1, LANES), lambda b, c: (b, 0, 0)),
        ],
        out_shape=[
            jax.ShapeDtypeStruct((nb, L, vt), BF16),
            jax.ShapeDtypeStruct((nb, MH, dqk, dv), F32),
            jax.ShapeDtypeStruct((nb, MH, dqk), F32),
            jax.ShapeDtypeStruct((nb, 1, LANES), F32),
        ],
        scratch_shapes=[
            pltpu.VMEM((MH, dqk, dv), F32),
            pltpu.VMEM((MH, dqk), F32),
            pltpu.VMEM((1, LANES), F32),
        ],
        compiler_params=_params(("parallel", "arbitrary")),
        name="mlstm_chunks",
    )(qk, v, og, gates, bias_row, g_head.reshape(1, vt), c0, n0, m0)


def _suffix_matrix():
    j = lax.broadcasted_iota(jnp.int32, (2 * SB_BLOCK, 2 * SB_BLOCK), 0) % SB_BLOCK
    s = lax.broadcasted_iota(jnp.int32, (2 * SB_BLOCK, 2 * SB_BLOCK), 1)
    return jnp.where((s >= SB_BLOCK) | (j > s), 1.0, 0.0).astype(BF16)


def _sb_scores(z, mask, suffix_mat):
    soft = jnp.log(1.0 + jnp.exp(-jnp.abs(z)))
    log_beta = jnp.minimum(z, 0.0) - soft
    log_1mb = log_beta - z
    if mask is not None:
        log_1mb = jnp.where(mask, log_1mb, 0.0)
    hi = log_1mb.astype(BF16)
    lo = (log_1mb - hi.astype(F32)).astype(BF16)
    sums = _dot(jnp.concatenate([hi, lo], axis=1), suffix_mat)
    return log_beta, sums[:, :SB_BLOCK], sums[:, SB_BLOCK:]


def _sb_visit(zs, masks, value_fns, carry, suffix_mat):
    scores = [_sb_scores(z, m, suffix_mat) for z, m in zip(zs, masks)]
    pv = None
    for (log_beta, later, total), mask, value_fn in zip(scores, masks, value_fns):
        a = jnp.exp(log_beta + later + carry)
        if mask is not None:
            a = jnp.where(mask, a, 0.0)
        out = value_fn(a.astype(BF16))
        pv = out if pv is None else pv + out
        carry = carry + total
    return pv, carry


def _sb_prompt_kernel(q_ref, k_ref, v_ref, bias_ref, o_ref, k_scr, v_scr, acc_scr, carry_scr,
                      *, n_qt, scale):
    k_scr[...] = k_ref[0].astype(BF16)
    v_scr[...] = v_ref[0].astype(BF16)
    suffix_mat = _suffix_matrix()
    bias = bias_ref[...]
    sub = SB_QTILE // SB_BLOCK

    def visit(r0, r1, blocks, mask):
        q = q_ref[0, r0:r1, :]
        zs, value_fns = [], []
        for j in blocks:
            keys = pl.ds(pl.multiple_of(j * SB_BLOCK, SB_BLOCK), SB_BLOCK)
            zs.append(_dot_nt(q, k_scr[keys, :]) * scale + bias)
            value_fns.append(lambda a, keys=keys: _dot(a, v_scr[keys, :]))
        pv, carry = _sb_visit(zs, [mask] * len(zs), value_fns, carry_scr[r0:r1, :], suffix_mat)
        acc_scr[r0:r1, :] += pv
        carry_scr[r0:r1, :] = carry

    acc_scr[...] = jnp.zeros_like(acc_scr)
    carry_scr[...] = jnp.zeros_like(carry_scr)
    for c in reversed(range(sub)):
        rows_left = SB_QTILE - c * SB_BLOCK
        row = lax.broadcasted_iota(jnp.int32, (rows_left, SB_BLOCK), 0)
        col = lax.broadcasted_iota(jnp.int32, (rows_left, SB_BLOCK), 1)
        for t in range(n_qt):
            visit(t * SB_QTILE + c * SB_BLOCK, (t + 1) * SB_QTILE, [t * sub + c], col < row)

    for t in range(1, n_qt):
        def older(i, _, t=t):
            newest = (t - i) * sub - 1
            visit(t * SB_QTILE, (t + 1) * SB_QTILE, [newest - d for d in range(sub)], None)
            return 0

        lax.fori_loop(0, t, older, 0)
    o_ref[0] = acc_scr[...].astype(BF16)


def _sb_prompt(q, k, v, bias_lanes):
    nb, L, hd = q.shape
    dh = hd // SB_HEADS
    spec = pl.BlockSpec((1, L, dh), lambda b, h: (b, 0, h))
    return pl.pallas_call(
        functools.partial(_sb_prompt_kernel, n_qt=L // SB_QTILE, scale=dh ** -0.5),
        grid=(nb, SB_HEADS),
        in_specs=[spec, spec, spec, pl.BlockSpec((1, dh), lambda b, h: (0, h))],
        out_specs=spec,
        out_shape=jax.ShapeDtypeStruct((nb, L, hd), BF16),
        scratch_shapes=[pltpu.VMEM((L, dh), BF16), pltpu.VMEM((L, dh), BF16),
                        pltpu.VMEM((L, dh), F32), pltpu.VMEM((L, SB_BLOCK), F32)],
        compiler_params=_params(("parallel", "parallel")),
        name="stickbreak_prompt",
    )(q, k, v, bias_lanes)


def _sb_decode_kernel(pt_ref, q_ref, bias_ref, kn_ref, vn_ref, ka_ref, va_ref, kb_ref, vb_ref, o_ref,
                      carry_scr, acc_scr, qbd_scr, expand_scr, hmask_scr, *, n_steps, n_new, scale):
    p = pl.program_id(1)
    rows = SB_HEADS * n_new
    dh = q_ref.shape[2] // SB_HEADS
    suffix_mat = _suffix_matrix()

    @pl.when(p == 0)
    def _():
        carry_scr[...] = jnp.zeros_like(carry_scr)
        acc_scr[...] = jnp.zeros_like(acc_scr)
        q_rep = jnp.concatenate([q_ref[0]] * SB_HEADS, axis=0)
        lane_head = lax.broadcasted_iota(jnp.int32, q_rep.shape, 1) // dh
        row_head = lax.broadcasted_iota(jnp.int32, q_rep.shape, 0) // n_new
        qbd_scr[...] = jnp.where(lane_head == row_head, q_rep, 0.0).astype(BF16)
        tok = lax.broadcasted_iota(jnp.int32, expand_scr.shape, 0)
        c = lax.broadcasted_iota(jnp.int32, expand_scr.shape, 1)
        expand_scr[...] = jnp.where(c // SB_HEADS == tok, 1.0, 0.0).astype(BF16)
        r = lax.broadcasted_iota(jnp.int32, hmask_scr.shape, 0)
        c = lax.broadcasted_iota(jnp.int32, hmask_scr.shape, 1)
        hmask_scr[...] = jnp.where(c % SB_HEADS == r // n_new, 1.0, 0.0).astype(BF16)

    def attend(kv_refs, mask):
        zs, value_fns = [], []
        for k_ref, v_ref in kv_refs:
            k_all = jnp.concatenate(
                [k_ref[pl.ds(h, PAGE_SIZE, stride=SB_HEADS), :].astype(BF16) for h in range(SB_HEADS)],
                axis=1)
            zs.append(_dot_nt(qbd_scr[...], k_all) * scale + bias_ref[...])

            def values(a, v_ref=v_ref):
                spread = _dot(a, expand_scr[...]).astype(BF16) * hmask_scr[...]
                return _dot(spread, v_ref[...].astype(BF16))

            value_fns.append(values)
        pv, carry = _sb_visit(zs, [mask] * len(zs), value_fns, carry_scr[...], suffix_mat)
        acc_scr[...] += pv
        carry_scr[...] = carry

    @pl.when(p == 0)
    def _():
        row = lax.broadcasted_iota(jnp.int32, (rows, SB_BLOCK), 0)
        col = lax.broadcasted_iota(jnp.int32, (rows, SB_BLOCK), 1)
        attend([(kn_ref, vn_ref)], col < row % n_new)

    @pl.when(p > 0)
    def _():
        attend([(ka_ref, va_ref), (kb_ref, vb_ref)], None)

    @pl.when(p == n_steps - 1)
    def _():
        o_ref[0] = acc_scr[...]


def _sb_decode(page_table, q, bias_rows, k_new, v_new, pool_k, pool_v, layer):
    nb, n_new, hd = q.shape
    dh = hd // SB_HEADS
    n_pages = page_table.shape[1]
    rows = SB_HEADS * n_new
    prow = PAGE_SIZE * SB_HEADS
    assert n_pages % 2 == 0 and rows == SB_BLOCK and PAGE_SIZE == SB_BLOCK
    n_steps = n_pages // 2 + 1
    n_layers, n_pool = pool_v.shape[:2]
    pool_k = pool_k.reshape(n_layers, n_pool, prow, dh)
    pool_v = pool_v.reshape(n_layers, n_pool, prow, dh)
    k_new = k_new.reshape(nb, prow, dh)
    v_new = v_new.reshape(nb, prow, dh)

    def page_spec(slot):
        return pl.BlockSpec(
            (None, None, prow, dh),
            lambda b, p, pt: (layer, pt[b, n_pages - 2 * jnp.maximum(p, 1) + 1 - slot], 0, 0))

    new_spec = pl.BlockSpec((None, prow, dh), lambda b, p, pt: (b, 0, 0))
    grid_spec = pltpu.PrefetchScalarGridSpec(
        num_scalar_prefetch=1,
        grid=(nb, n_steps),
        in_specs=[
            pl.BlockSpec((1, n_new, hd), lambda b, p, pt: (b, 0, 0)),
            pl.BlockSpec((rows, SB_BLOCK), lambda b, p, pt: (0, 0)),
            new_spec, new_spec,
            page_spec(0), page_spec(0), page_spec(1), page_spec(1),
        ],
        out_specs=pl.BlockSpec((1, rows, dh), lambda b, p, pt: (b, 0, 0)),
        scratch_shapes=[pltpu.VMEM((rows, SB_BLOCK), F32), pltpu.VMEM((rows, dh), F32),
                        pltpu.VMEM((rows, hd), BF16), pltpu.VMEM((PAGE_SIZE, prow), BF16),
                        pltpu.VMEM((rows, prow), BF16)],
    )
    return pl.pallas_call(
        functools.partial(_sb_decode_kernel, n_steps=n_steps, n_new=n_new, scale=dh ** -0.5),
        grid_spec=grid_spec,
        out_shape=jax.ShapeDtypeStruct((nb, rows, dh), F32),
        compiler_params=_params(("parallel", "arbitrary")),
        name="stickbreak_decode",
    )(page_table, q, bias_rows, k_new, v_new, pool_k, pool_v, pool_k, pool_v)


def _ffn(x, mods, g_pre, g_post, w_in, w_out, layer, tf=512):
    d_ff = w_out.shape[1]
    (act,) = _modmm(x, g_pre, mods[3], mods[4], w_in, layer, (0, d_ff), tf, d_ff // tf,
                    (BF16,), _swiglu_epilogue)
    return _mm_norm_res(act, w_out, layer, x, mods[5], g_post)


def _mlstm_layer(x, mods, g_norm, w_in, w_out, layer, b_gate, g_head, c0, n0, m0, chunk_rows):
    D = x.shape[2]
    qk_tot = (w_in.shape[2] - 2 * MH) // 3
    dqk = qk_tot // (2 * MH)
    tn = PROJ_TILE
    w_gate = jnp.zeros((D, 2 * LANES), F32)
    w_gate = w_gate.at[:, :MH].set(w_in[layer, :, 3 * qk_tot:3 * qk_tot + MH])
    w_gate = w_gate.at[:, LANES:LANES + MH].set(w_in[layer, :, 3 * qk_tot + MH:])
    bias_row = jnp.zeros((1, 2 * LANES), F32)
    bias_row = bias_row.at[0, :MH].set(b_gate[0].astype(F32))
    bias_row = bias_row.at[0, LANES:LANES + MH].set(b_gate[1].astype(F32))
    epilogue = [functools.partial(_scaled_q_then_k, q_tiles=(MH * dqk) // tn, q_scale=dqk ** -0.5),
                _as_bf16, _as_f32]
    qk, v, og, gates = _modmm(x, g_norm[0], mods[0], mods[1], w_in, layer, (0, qk_tot, 2 * qk_tot), tn,
                              qk_tot // tn, (BF16, BF16, F32), epilogue, w_gate=w_gate)
    m0p = jnp.zeros((m0.shape[0], 1, LANES), F32).at[:, 0, :MH].set(m0.astype(F32))
    if chunk_rows == MLSTM_CHUNK:
        h, c_new, n_new, m_new = _mlstm(qk, v, og, gates, bias_row, g_head,
                                        c0.astype(F32), n0.astype(F32), m0p, MLSTM_CHUNK)
    else:
        nb = x.shape[1] // chunk_rows

        def pad(a):
            a = a.reshape(nb, chunk_rows, a.shape[2])
            return jnp.pad(a, ((0, 0), (0, MLSTM_CHUNK - chunk_rows), (0, 0)))

        h, c_new, n_new, m_new = _mlstm(pad(qk), pad(v), pad(og), pad(gates), bias_row, g_head,
                                        c0.astype(F32), n0.astype(F32), m0p, chunk_rows)
        h = h[:, :chunk_rows].reshape(1, nb * chunk_rows, h.shape[2])
    x = _mm_norm_res(h, w_out, layer, x, mods[2], g_norm[1])
    return x, c_new, n_new, m_new[:, 0, :MH]


def kernel(x_prompt, x_sample, state_C, state_n, state_m, cache_k, cache_v, page_table, c_prompt, c_sample, w_ada, b_ada, g_norm, w_in_a, b_gate_a, g_head_a, w_out_a, w_in_b, b_sb, w_out_b, w_ffn_in, w_ffn_out):
    bp, seq, D = x_prompt.shape
    bs, dec_seq, _ = x_sample.shape
    depth = w_ada.shape[0]
    dh = D // SB_HEADS

    rows = -(-(bp + bs) // 8) * 8
    c_all = jnp.zeros((rows, D), F32).at[:bp].set(c_prompt).at[bp:bp + bs].set(c_sample)
    ada = _ada(c_all, w_ada, b_ada)

    xp = x_prompt
    xs = x_sample.reshape(1, bs * dec_seq, D)
    outs = {name: [] for name in ("kp", "vp", "ks", "vs", "Cp", "np", "mp", "Cs", "ns", "ms")}
    for i in range(depth):
        j = i // 2
        mods_p = [ada[i, :bp, s * D:(s + 1) * D].reshape(bp, 1, D) for s in range(6)]
        mods_s = [jnp.repeat(ada[i, bp:bp + bs, s * D:(s + 1) * D], dec_seq, axis=0).reshape(1, bs * dec_seq, D)
                  for s in range(6)]
        if i % 2 == 0:
            dqk = (w_in_a.shape[2] - 2 * MH) // 3 // (2 * MH)
            dv = 2 * dqk
            zc = jnp.zeros((bp, MH, dqk, dv), F32)
            zn = jnp.zeros((bp, MH, dqk), F32)
            zm = jnp.zeros((bp, MH), F32)
            xp, Cp, n_p, m_p = _mlstm_layer(xp, mods_p, g_norm[i], w_in_a, w_out_a, j, b_gate_a[j], g_head_a[j],
                                            zc, zn, zm, MLSTM_CHUNK)
            xs, Cs, n_s, m_s = _mlstm_layer(xs, mods_s, g_norm[i], w_in_a, w_out_a, j, b_gate_a[j], g_head_a[j],
                                            state_C[j], state_n[j], state_m[j], dec_seq)
            outs["Cp"].append(Cp); outs["np"].append(n_p); outs["mp"].append(m_p)
            outs["Cs"].append(Cs); outs["ns"].append(n_s); outs["ms"].append(m_s)
        else:
            tn = PROJ_TILE
            bias = b_sb[j].astype(F32)
            qp, kp, vp = _modmm(xp, g_norm[i, 0], mods_p[0], mods_p[1], w_in_b, j, (0, D, 2 * D), tn,
                                D // tn, (BF16, F32, F32), [_as_bf16, _as_f32, _as_f32])
            op = _sb_prompt(qp, kp, vp, jnp.repeat(bias, dh).reshape(1, D))
            xp = _mm_norm_res(op, w_out_b, j, xp, mods_p[2], g_norm[i, 1])
            qs, kn, vn = _modmm(xs, g_norm[i, 0], mods_s[0], mods_s[1], w_in_b, j, (0, D, 2 * D), tn,
                                D // tn, (F32, F32, F32), [_as_f32, _as_f32, _as_f32])

            def as_page(a):
                a = a.reshape(bs, dec_seq, SB_HEADS, dh)
                return jnp.pad(a, ((0, 0), (0, PAGE_SIZE - dec_seq), (0, 0), (0, 0)))

            os_ = _sb_decode(page_table, qs.reshape(bs, dec_seq, D),
                             jnp.broadcast_to(jnp.repeat(bias, dec_seq)[:, None], (SB_HEADS * dec_seq, SB_BLOCK)),
                             as_page(kn), as_page(vn), cache_k, cache_v, j)
            os_ = os_.reshape(bs, SB_HEADS, dec_seq, dh).transpose(0, 2, 1, 3).reshape(1, bs * dec_seq, D)
            xs = _mm_norm_res(os_.astype(BF16), w_out_b, j, xs, mods_s[2], g_norm[i, 1])
            outs["kp"].append(kp.reshape(bp, seq, SB_HEADS, dh)); outs["vp"].append(vp.reshape(bp, seq, SB_HEADS, dh))
            outs["ks"].append(kn.reshape(bs, dec_seq, SB_HEADS, dh)); outs["vs"].append(vn.reshape(bs, dec_seq, SB_HEADS, dh))
        xp = _ffn(xp, mods_p, g_norm[i, 2], g_norm[i, 3], w_ffn_in, w_ffn_out, i)
        xs = _ffn(xs, mods_s, g_norm[i, 2], g_norm[i, 3], w_ffn_in, w_ffn_out, i)

    st = lambda name: jnp.stack(outs[name])
    return (xp, xs.reshape(bs, dec_seq, D), st("kp"), st("vp"), st("ks"), st("vs"),
            st("Cp"), st("np"), st("mp"), st("Cs"), st("ns"), st("ms"))
```

```python
import functools

import jax
import jax.numpy as jnp
from jax import lax
from jax.experimental import pallas as pl
from jax.experimental.pallas import tpu as pltpu

F32 = jnp.float32
BF16 = jnp.bfloat16

EPS = 1e-6
MH = 8
MLSTM_CHUNK = 128
SB_HEADS = 16
SB_BLOCK = 128
SB_QTILE = 512
PAGE_SIZE = 128
NEG_BIG = -1e30

LANES = 128
SUBLANES = 8
VMEM_LIMIT = 52 * 1024 * 1024
ACC_COLS = 512
PROJ_TILE = 512
DECODE_PAGES = 4


def _dot(a, b):
    return jnp.dot(a, b, preferred_element_type=F32)


def _dot_nt(a, b):
    return lax.dot_general(a, b, (((1,), (1,)), ((), ())), preferred_element_type=F32)


def _dot_tn(a, b):
    return lax.dot_general(a, b, (((0,), (0,)), ((), ())), preferred_element_type=F32)


def _params(sem):
    return pltpu.CompilerParams(dimension_semantics=sem, vmem_limit_bytes=VMEM_LIMIT)


def _ada_kernel(c_ref, w_ref, b_ref, o_ref):
    c = c_ref[...]
    a = (c * jax.nn.sigmoid(c)).astype(BF16)
    o_ref[0] = _dot(a, w_ref[0].astype(BF16)) + b_ref[0]


def _ada(c_all, w_ada, b_ada, tn=1024):
    depth, d, n = w_ada.shape
    rows = c_all.shape[0]
    return pl.pallas_call(
        _ada_kernel,
        grid=(depth, n // tn),
        in_specs=[
            pl.BlockSpec((rows, d), lambda l, j: (0, 0)),
            pl.BlockSpec((1, d, tn), lambda l, j: (l, 0, j)),
            pl.BlockSpec((1, 1, tn), lambda l, j: (l, 0, j)),
        ],
        out_specs=pl.BlockSpec((1, rows, tn), lambda l, j: (l, 0, j)),
        out_shape=jax.ShapeDtypeStruct((depth, rows, n), F32),
        compiler_params=_params(("parallel", "parallel")),
        name="ada_params",
    )(c_all, w_ada, b_ada.reshape(depth, 1, n))


def _row_chunks(tl):
    rc = min(tl, 256)
    return rc, tl // rc


def _modulated(x, g, scale, shift):
    r = lax.rsqrt(jnp.mean(x * x, axis=-1, keepdims=True) + EPS)
    return ((x * r) * g) * (1.0 + scale) + shift


def _modmm_kernel(*refs, n_slices, n_tiles, epilogue, sequential, has_gate, tl, per_row):
    x_ref, g_ref, sh_ref, sc_ref = refs[:4]
    n_w = 1 if sequential else n_slices
    w_refs = refs[4:4 + n_w]
    pos = 4 + n_w
    if has_gate:
        wg_ref = refs[pos]
        pos += 1
    n_scratch = 2 if sequential else 1
    n_out = len(refs) - pos - n_scratch - (1 if has_gate else 0)
    out_refs = refs[pos:pos + n_out]
    pos += n_out
    if has_gate:
        go_ref = refs[pos]
        pos += 1
    h_scr = refs[pos]
    y_scr = refs[pos + 1] if sequential else None
    j = pl.program_id(2)

    @pl.when(j == 0)
    def _():
        rc, n_chunks = _row_chunks(tl)

        def chunk(c):
            rows = pl.ds(pl.multiple_of(c * rc, rc), rc)
            x = x_ref[0, rows, :]
            if per_row:
                sc, sh = sc_ref[0, rows, :], sh_ref[0, rows, :]
            else:
                sc, sh = sc_ref[0], sh_ref[0]
            h_scr[rows, :] = _modulated(x, g_ref[0], sc, sh).astype(BF16)

        if n_chunks == 1:
            chunk(0)
        else:
            pl.loop(0, n_chunks)(chunk)
        if has_gate:
            go_ref[0] = _dot(h_scr[...], wg_ref[...].astype(BF16))

    if sequential:
        y_scr[...] = _dot(h_scr[...], w_refs[0][...].astype(BF16))
        for s, out_ref in enumerate(out_refs):
            @pl.when(j // n_tiles == s)
            def _(s=s, out_ref=out_ref):
                out_ref[0] = epilogue[s](y_scr[...], j - s * n_tiles)
    else:
        hb = h_scr[...]
        ys = [_dot(hb, w_ref[...].astype(BF16)) for w_ref in w_refs]
        epilogue(ys, out_refs, j)


def _modmm(x, g, shift, scale, w, layer, slice_starts, tn, n_tiles, out_dtypes, epilogue,
           w_gate=None, tl=1024):
    nb, L, D = x.shape
    tl = min(tl, L)
    per_row = shift.shape[1] != 1
    r = tl if per_row else 1
    n_slices = len(slice_starts)
    has_gate = w_gate is not None
    sequential = isinstance(epilogue, (list, tuple))
    if sequential:
        assert all(s0 == slice_starts[0] + s * n_tiles * tn for s, s0 in enumerate(slice_starts))

    def mod_map(b, i, j):
        return (b, i if per_row else 0, 0)

    in_specs = [
        pl.BlockSpec((1, tl, D), lambda b, i, j: (b, i, 0)),
        pl.BlockSpec((1, 1, D), lambda b, i, j: (0, 0, 0)),
        pl.BlockSpec((1, r, D), mod_map),
        pl.BlockSpec((1, r, D), mod_map),
    ]
    args = [x, g.reshape(1, 1, D), shift, scale]
    for s0 in slice_starts[:1] if sequential else slice_starts:
        off = s0 // tn
        in_specs.append(pl.BlockSpec((None, D, tn), lambda b, i, j, off=off: (layer, 0, off + j)))
        args.append(w)
    if has_gate:
        gw = w_gate.shape[1]
        in_specs.append(pl.BlockSpec((D, gw), lambda b, i, j: (0, 0)))
        args.append(w_gate)
    if sequential:
        out_specs = [
            pl.BlockSpec((1, tl, tn), lambda b, i, j, s=s: (b, i, jnp.clip(j - s * n_tiles, 0, n_tiles - 1)))
            for s in range(n_slices)]
    else:
        out_specs = [pl.BlockSpec((1, tl, tn), lambda b, i, j: (b, i, j)) for _ in out_dtypes]
    out_shape = [jax.ShapeDtypeStruct((nb, L, n_tiles * tn), dt) for dt in out_dtypes]
    if has_gate:
        out_specs.append(pl.BlockSpec((1, tl, gw), lambda b, i, j: (b, i, 0)))
        out_shape.append(jax.ShapeDtypeStruct((nb, L, gw), F32))
    return pl.pallas_call(
        functools.partial(_modmm_kernel, n_slices=n_slices, n_tiles=n_tiles, epilogue=epilogue,
                          sequential=sequential, has_gate=has_gate, tl=tl, per_row=per_row),
        grid=(nb, L // tl, n_tiles * n_slices if sequential else n_tiles),
        in_specs=in_specs,
        out_specs=out_specs,
        out_shape=out_shape,
        scratch_shapes=[pltpu.VMEM((tl, D), BF16)] + ([pltpu.VMEM((tl, tn), F32)] if sequential else []),
        compiler_params=_params(("parallel", "parallel", "arbitrary")),
        name="modulated_matmul",
    )(*args)


def _as_bf16(y, tile):
    return y.astype(BF16)


def _as_f32(y, tile):
    return y


def _scaled_q_then_k(y, tile, *, q_tiles, q_scale):
    return (y * jnp.where(tile < q_tiles, q_scale, 1.0).astype(F32)).astype(BF16)


def _swiglu_epilogue(ys, out_refs, j):
    gate, up = ys
    out_refs[0][0] = ((gate * jax.nn.sigmoid(gate)) * up).astype(BF16)


def _mm_norm_res_kernel(a_ref, w_ref, x_ref, gate_ref, g_ref, o_ref, *, tl, nk, per_row):
    k = pl.program_id(2)

    @pl.when(k == 0)
    def _():
        o_ref[...] = jnp.zeros_like(o_ref)

    a = a_ref[0]
    cw = min(ACC_COLS, o_ref.shape[2])
    for c in range(o_ref.shape[2] // cw):
        cols = slice(c * cw, (c + 1) * cw)
        o_ref[0, :, cols] += _dot(a, w_ref[:, cols].astype(BF16))

    @pl.when(k == nk - 1)
    def _():
        rc, n_chunks = _row_chunks(tl)

        def chunk(c):
            rows = pl.ds(pl.multiple_of(c * rc, rc), rc)
            acc = o_ref[0, rows, :]
            r = lax.rsqrt(jnp.mean(acc * acc, axis=-1, keepdims=True) + EPS)
            gate = gate_ref[0, rows, :] if per_row else gate_ref[0]
            o_ref[0, rows, :] = x_ref[0, rows, :] + gate * ((acc * r) * g_ref[0])

        if n_chunks == 1:
            chunk(0)
        else:
            pl.loop(0, n_chunks)(chunk)


def _mm_norm_res(a, w, layer, x, gate, g, tk=512, tl=1024):
    nb, L, K = a.shape
    D = w.shape[2]
    tl = min(tl, L)
    per_row = gate.shape[1] != 1
    r = tl if per_row else 1
    nk = K // tk
    return pl.pallas_call(
        functools.partial(_mm_norm_res_kernel, tl=tl, nk=nk, per_row=per_row),
        grid=(nb, L // tl, nk),
        in_specs=[
            pl.BlockSpec((1, tl, tk), lambda b, i, k: (b, i, k)),
            pl.BlockSpec((None, tk, D), lambda b, i, k: (layer, k, 0)),
            pl.BlockSpec((1, tl, D), lambda b, i, k: (b, i, 0)),
            pl.BlockSpec((1, r, D), lambda b, i, k: (b, i if per_row else 0, 0)),
            pl.BlockSpec((1, 1, D), lambda b, i, k: (0, 0, 0)),
        ],
        out_specs=pl.BlockSpec((1, tl, D), lambda b, i, k: (b, i, 0)),
        out_shape=jax.ShapeDtypeStruct((nb, L, D), F32),
        compiler_params=_params(("parallel", "parallel", "arbitrary")),
        name="matmul_norm_residual",
    )(a, w, x, gate, g.reshape(1, 1, D))


def _scan_rows(x, op, row):
    d = 1
    while d < x.shape[0]:
        shifted = pltpu.roll(x, d, axis=0)
        x = jnp.where(row >= d, op(x, shifted), x)
        d *= 2
    return x


def _mlstm_kernel(qk_ref, v_ref, og_ref, gt_ref, bias_ref, gh_ref, c0_ref, n0_ref, m0_ref,
                  o_ref, c_out_ref, n_out_ref, m_out_ref, c_scr, n_scr, m_scr,
                  *, valid_len, nc, dqk, dv):
    c = pl.program_id(1)
    L = MLSTM_CHUNK

    @pl.when(c == 0)
    def _():
        c_scr[...] = c0_ref[0]
        n_scr[...] = n0_ref[0]
        m_scr[...] = m0_ref[0]

    gates = gt_ref[0] + bias_ref[...]
    li = gates[:, :LANES]
    gf = gates[:, LANES:]
    lf = jnp.minimum(gf, 0.0) - jnp.log1p(jnp.exp(-jnp.abs(gf)))
    row = lax.broadcasted_iota(jnp.int32, (L, LANES), 0)
    col = lax.broadcasted_iota(jnp.int32, (L, LANES), 1)
    if valid_len < L:
        li = jnp.where(row < valid_len, li, NEG_BIG)
        lf = jnp.where(row < valid_len, lf, 0.0)

    b = _scan_rows(lf, jnp.add, row)
    u = li - b
    m_prev = m_scr[...]
    m_t = b + jnp.maximum(m_prev, _scan_rows(u, jnp.maximum, row))
    d_inter = jnp.exp(b + m_prev - m_t)
    ct = b - m_t
    inv_floor = jnp.exp(-m_t)
    b_end = b[L - 1:L, :]
    m_end = m_t[L - 1:L, :]
    w_all = jnp.exp(b_end - b + li - m_end)
    decay = jnp.exp(b_end + m_prev - m_end)
    u_t = u.T
    causal = row >= col

    for h in range(MH):
        q = qk_ref[0, :, h * dqk:(h + 1) * dqk]
        k = qk_ref[0, :, MH * dqk + h * dqk:MH * dqk + (h + 1) * dqk]
        v = v_ref[0, :, h * dv:(h + 1) * dv]
        log_d = jnp.where(causal, ct[:, h:h + 1] + u_t[h:h + 1, :], NEG_BIG)
        s = _dot_nt(q, k) * jnp.exp(log_d)
        di = d_inter[:, h:h + 1]
        c_prev = c_scr[h]
        num = _dot(s.astype(BF16), v) + di * _dot(q, c_prev.astype(BF16))
        n_prev = n_scr[h:h + 1, :]
        qn_state = jnp.sum(q.astype(F32) * n_prev.astype(BF16).astype(F32), axis=-1, keepdims=True)
        qn = jnp.sum(s, axis=-1, keepdims=True) + di * qn_state
        hh = num / jnp.maximum(jnp.abs(qn), inv_floor[:, h:h + 1])
        hh = hh * lax.rsqrt(jnp.mean(hh * hh, axis=-1, keepdims=True) + EPS)
        hh = hh * gh_ref[:, h * dv:(h + 1) * dv]
        og = og_ref[0, :, h * dv:(h + 1) * dv]
        o_ref[0, :, h * dv:(h + 1) * dv] = (hh * jax.nn.sigmoid(og)).astype(BF16)

        kw = k.astype(F32) * w_all[:, h:h + 1]
        dec = decay[:, h:h + 1]
        c_scr[h] = dec * c_prev + _dot_tn(kw.astype(BF16), v)
        n_scr[h:h + 1, :] = dec * n_prev + jnp.sum(kw, axis=0, keepdims=True)

    m_scr[...] = m_end

    @pl.when(c == nc - 1)
    def _():
        c_out_ref[0] = c_scr[...]
        n_out_ref[0] = n_scr[...]
        m_out_ref[0] = m_scr[...]


def _mlstm(qk, v, og, gates, bias_row, g_head, c0, n0, m0, valid_len):
    nb, L, vt = v.shape
    dv = vt // MH
    dqk = qk.shape[2] // (2 * MH)
    nc = L // MLSTM_CHUNK
    blk = lambda width: pl.BlockSpec((1, MLSTM_CHUNK, width), lambda b, c: (b, c, 0))
    return pl.pallas_call(
        functools.partial(_mlstm_kernel, valid_len=valid_len, nc=nc, dqk=dqk, dv=dv),
        grid=(nb, nc),
        in_specs=[
            blk(qk.shape[2]), blk(vt), blk(vt), blk(2 * LANES),
            pl.BlockSpec((1, 2 * LANES), lambda b, c: (0, 0)),
            pl.BlockSpec((1, vt), lambda b, c: (0, 0)),
            pl.BlockSpec((1, MH, dqk, dv), lambda b, c: (b, 0, 0, 0)),
            pl.BlockSpec((1, MH, dqk), lambda b, c: (b, 0, 0)),
            pl.BlockSpec((1, 1, LANES), lambda b, c: (b, 0, 0)),
        ],
        out_specs=[
            blk(vt),
            pl.BlockSpec((1, MH, dqk, dv), lambda b, c: (b, 0, 0, 0)),
            pl.BlockSpec((1, MH, dqk), lambda b, c: (b, 0, 0)),
            pl.BlockSpec((1, 1, LANES), lambda b, c: (b, 0, 0)),
        ],
        out_shape=[
            jax.ShapeDtypeStruct((nb, L, vt), BF16),
            jax.ShapeDtypeStruct((nb, MH, dqk, dv), F32),
            jax.ShapeDtypeStruct((nb, MH, dqk), F32),
            jax.ShapeDtypeStruct((nb, 1, LANES), F32),
        ],
        scratch_shapes=[
            pltpu.VMEM((MH, dqk, dv), F32),
            pltpu.VMEM((MH, dqk), F32),
            pltpu.VMEM((1, LANES), F32),
        ],
        compiler_params=_params(("parallel", "arbitrary")),
        name="mlstm_chunks",
    )(qk, v, og, gates, bias_row, g_head.reshape(1, vt), c0, n0, m0)


def _suffix_matrix():
    j = lax.broadcasted_iota(jnp.int32, (2 * SB_BLOCK, 2 * SB_BLOCK), 0) % SB_BLOCK
    s = lax.broadcasted_iota(jnp.int32, (2 * SB_BLOCK, 2 * SB_BLOCK), 1)
    return jnp.where((s >= SB_BLOCK) | (j > s), 1.0, 0.0).astype(BF16)


def _sb_scores(z, mask, suffix_mat):
    soft = jnp.log(1.0 + jnp.exp(-jnp.abs(z)))
    log_beta = jnp.minimum(z, 0.0) - soft
    log_1mb = log_beta - z
    if mask is not None:
        log_1mb = jnp.where(mask, log_1mb, 0.0)
    hi = log_1mb.astype(BF16)
    lo = (log_1mb - hi.astype(F32)).astype(BF16)
    sums = _dot(jnp.concatenate([hi, lo], axis=1), suffix_mat)
    return log_beta, sums[:, :SB_BLOCK], sums[:, SB_BLOCK:]


def _sb_visit(zs, masks, value_fns, carry, suffix_mat):
    scores = [_sb_scores(z, m, suffix_mat) for z, m in zip(zs, masks)]
    pv = None
    for (log_beta, later, total), mask, value_fn in zip(scores, masks, value_fns):
        a = jnp.exp(log_beta + later + carry)
        if mask is not None:
            a = jnp.where(mask, a, 0.0)
        out = value_fn(a.astype(BF16))
        pv = out if pv is None else pv + out
        carry = carry + total
    return pv, carry


def _sb_prompt_kernel(q_ref, k_ref, v_ref, bias_ref, o_ref, k_scr, v_scr, acc_scr, carry_scr,
                      *, n_qt, scale):
    k_scr[...] = k_ref[0].astype(BF16)
    v_scr[...] = v_ref[0].astype(BF16)
    suffix_mat = _suffix_matrix()
    bias = bias_ref[...]
    sub = SB_QTILE // SB_BLOCK

    def visit(r0, r1, blocks, mask):
        q = q_ref[0, r0:r1, :]
        zs, value_fns = [], []
        for j in blocks:
            keys = pl.ds(pl.multiple_of(j * SB_BLOCK, SB_BLOCK), SB_BLOCK)
            zs.append(_dot_nt(q, k_scr[keys, :]) * scale + bias)
            value_fns.append(lambda a, keys=keys: _dot(a, v_scr[keys, :]))
        pv, carry = _sb_visit(zs, [mask] * len(zs), value_fns, carry_scr[r0:r1, :], suffix_mat)
        acc_scr[r0:r1, :] += pv
        carry_scr[r0:r1, :] = carry

    acc_scr[...] = jnp.zeros_like(acc_scr)
    carry_scr[...] = jnp.zeros_like(carry_scr)
    for c in reversed(range(sub)):
        rows_left = SB_QTILE - c * SB_BLOCK
        row = lax.broadcasted_iota(jnp.int32, (rows_left, SB_BLOCK), 0)
        col = lax.broadcasted_iota(jnp.int32, (rows_left, SB_BLOCK), 1)
        for t in range(n_qt):
            visit(t * SB_QTILE + c * SB_BLOCK, (t + 1) * SB_QTILE, [t * sub + c], col < row)

    for t in range(1, n_qt):
        def older(i, _, t=t):
            newest = (t - i) * sub - 1
            visit(t * SB_QTILE, (t + 1) * SB_QTILE, [newest - d for d in range(sub)], None)
            return 0

        lax.fori_loop(0, t, older, 0)
    o_ref[0] = acc_scr[...].astype(BF16)


def _sb_prompt(q, k, v, bias_lanes):
    nb, L, hd = q.shape
    dh = hd // SB_HEADS
    spec = pl.BlockSpec((1, L, dh), lambda b, h: (b, 0, h))
    return pl.pallas_call(
        functools.partial(_sb_prompt_kernel, n_qt=L // SB_QTILE, scale=dh ** -0.5),
        grid=(nb, SB_HEADS),
        in_specs=[spec, spec, spec, pl.BlockSpec((1, dh), lambda b, h: (0, h))],
        out_specs=spec,
        out_shape=jax.ShapeDtypeStruct((nb, L, hd), BF16),
        scratch_shapes=[pltpu.VMEM((L, dh), BF16), pltpu.VMEM((L, dh), BF16),
                        pltpu.VMEM((L, dh), F32), pltpu.VMEM((L, SB_BLOCK), F32)],
        compiler_params=_params(("parallel", "parallel")),
        name="stickbreak_prompt",
    )(q, k, v, bias_lanes)


def _sb_decode_kernel(pt_ref, q_ref, bias_ref, kn_ref, vn_ref, *rest, n_steps, n_new, scale):
    page_refs = rest[:2 * DECODE_PAGES]
    o_ref, carry_scr, acc_scr, qbd_scr, expand_scr, hmask_scr = rest[2 * DECODE_PAGES:]
    p = pl.program_id(1)
    rows = SB_HEADS * n_new
    dh = q_ref.shape[2] // SB_HEADS
    suffix_mat = _suffix_matrix()

    @pl.when(p == 0)
    def _():
        carry_scr[...] = jnp.zeros_like(carry_scr)
        acc_scr[...] = jnp.zeros_like(acc_scr)
        q_rep = jnp.concatenate([q_ref[0]] * SB_HEADS, axis=0)
        lane_head = lax.broadcasted_iota(jnp.int32, q_rep.shape, 1) // dh
        row_head = lax.broadcasted_iota(jnp.int32, q_rep.shape, 0) // n_new
        qbd_scr[...] = jnp.where(lane_head == row_head, q_rep, 0.0).astype(BF16)
        tok = lax.broadcasted_iota(jnp.int32, expand_scr.shape, 0)
        c = lax.broadcasted_iota(jnp.int32, expand_scr.shape, 1)
        expand_scr[...] = jnp.where(c // SB_HEADS == tok, 1.0, 0.0).astype(BF16)
        r = lax.broadcasted_iota(jnp.int32, hmask_scr.shape, 0)
        c = lax.broadcasted_iota(jnp.int32, hmask_scr.shape, 1)
        hmask_scr[...] = jnp.where(c % SB_HEADS == r // n_new, 1.0, 0.0).astype(BF16)

    def attend(kv_refs, mask):
        zs, value_fns = [], []
        for k_ref, v_ref in kv_refs:
            k_all = jnp.concatenate(
                [k_ref[pl.ds(h, PAGE_SIZE, stride=SB_HEADS), :].astype(BF16) for h in range(SB_HEADS)],
                axis=1)
            zs.append(_dot_nt(qbd_scr[...], k_all) * scale + bias_ref[...])

            def values(a, v_ref=v_ref):
                spread = _dot(a, expand_scr[...]).astype(BF16) * hmask_scr[...]
                return _dot(spread, v_ref[...].astype(BF16))

            value_fns.append(values)
        pv, carry = _sb_visit(zs, [mask] * len(zs), value_fns, carry_scr[...], suffix_mat)
        acc_scr[...] += pv
        carry_scr[...] = carry

    @pl.when(p == 0)
    def _():
        row = lax.broadcasted_iota(jnp.int32, (rows, SB_BLOCK), 0)
        col = lax.broadcasted_iota(jnp.int32, (rows, SB_BLOCK), 1)
        attend([(kn_ref, vn_ref)], col < row % n_new)

    @pl.when(p > 0)
    def _():
        attend([(page_refs[2 * i], page_refs[2 * i + 1]) for i in range(DECODE_PAGES)], None)

    @pl.when(p == n_steps - 1)
    def _():
        o_ref[0] = acc_scr[...]


def _sb_decode(page_table, q, bias_rows, k_new, v_new, pool_k, pool_v, layer):
    nb, n_new, hd = q.shape
    dh = hd // SB_HEADS
    n_pages = page_table.shape[1]
    rows = SB_HEADS * n_new
    prow = PAGE_SIZE * SB_HEADS
    assert n_pages % DECODE_PAGES == 0 and rows == SB_BLOCK and PAGE_SIZE == SB_BLOCK
    n_steps = n_pages // DECODE_PAGES + 1
    n_layers, n_pool = pool_v.shape[:2]
    pool_k = pool_k.reshape(n_layers, n_pool, prow, dh)
    pool_v = pool_v.reshape(n_layers, n_pool, prow, dh)
    k_new = k_new.reshape(nb, prow, dh)
    v_new = v_new.reshape(nb, prow, dh)

    def page_spec(slot):
        return pl.BlockSpec(
            (None, None, prow, dh),
            lambda b, p, pt: (layer, pt[b, n_pages - DECODE_PAGES * jnp.maximum(p, 1) + DECODE_PAGES - 1 - slot], 0, 0))

    new_spec = pl.BlockSpec((None, prow, dh), lambda b, p, pt: (b, 0, 0))
    grid_spec = pltpu.PrefetchScalarGridSpec(
        num_scalar_prefetch=1,
        grid=(nb, n_steps),
        in_specs=[
            pl.BlockSpec((1, n_new, hd), lambda b, p, pt: (b, 0, 0)),
            pl.BlockSpec((rows, SB_BLOCK), lambda b, p, pt: (0, 0)),
            new_spec, new_spec,
        ] + [page_spec(slot) for slot in range(DECODE_PAGES) for _ in range(2)],
        out_specs=pl.BlockSpec((1, rows, dh), lambda b, p, pt: (b, 0, 0)),
        scratch_shapes=[pltpu.VMEM((rows, SB_BLOCK), F32), pltpu.VMEM((rows, dh), F32),
                        pltpu.VMEM((rows, hd), BF16), pltpu.VMEM((PAGE_SIZE, prow), BF16),
                        pltpu.VMEM((rows, prow), BF16)],
    )
    return pl.pallas_call(
        functools.partial(_sb_decode_kernel, n_steps=n_steps, n_new=n_new, scale=dh ** -0.5),
        grid_spec=grid_spec,
        out_shape=jax.ShapeDtypeStruct((nb, rows, dh), F32),
        compiler_params=_params(("parallel", "arbitrary")),
        name="stickbreak_decode",
    )(page_table, q, bias_rows, k_new, v_new, *([pool_k, pool_v] * DECODE_PAGES))


def _ffn(x, mods, g_pre, g_post, w_in, w_out, layer, tf=512):
    d_ff = w_out.shape[1]
    (act,) = _modmm(x, g_pre, mods[3], mods[4], w_in, layer, (0, d_ff), tf, d_ff // tf,
                    (BF16,), _swiglu_epilogue)
    return _mm_norm_res(act, w_out, layer, x, mods[5], g_post)


def _mlstm_layer(x, mods, g_norm, w_in, w_out, layer, b_gate, g_head, c0, n0, m0, chunk_rows):
    D = x.shape[2]
    qk_tot = (w_in.shape[2] - 2 * MH) // 3
    dqk = qk_tot // (2 * MH)
    tn = PROJ_TILE
    w_gate = jnp.zeros((D, 2 * LANES), F32)
    w_gate = w_gate.at[:, :MH].set(w_in[layer, :, 3 * qk_tot:3 * qk_tot + MH])
    w_gate = w_gate.at[:, LANES:LANES + MH].set(w_in[layer, :, 3 * qk_tot + MH:])
    bias_row = jnp.zeros((1, 2 * LANES), F32)
    bias_row = bias_row.at[0, :MH].set(b_gate[0].astype(F32))
    bias_row = bias_row.at[0, LANES:LANES + MH].set(b_gate[1].astype(F32))
    epilogue = [functools.partial(_scaled_q_then_k, q_tiles=(MH * dqk) // tn, q_scale=dqk ** -0.5),
                _as_bf16, _as_f32]
    qk, v, og, gates = _modmm(x, g_norm[0], mods[0], mods[1], w_in, layer, (0, qk_tot, 2 * qk_tot), tn,
                              qk_tot // tn, (BF16, BF16, F32), epilogue, w_gate=w_gate)
    m0p = jnp.zeros((m0.shape[0], 1, LANES), F32).at[:, 0, :MH].set(m0.astype(F32))
    if chunk_rows == MLSTM_CHUNK:
        h, c_new, n_new, m_new = _mlstm(qk, v, og, gates, bias_row, g_head,
                                        c0.astype(F32), n0.astype(F32), m0p, MLSTM_CHUNK)
    else:
        nb = x.shape[1] // chunk_rows

        def pad(a):
            a = a.reshape(nb, chunk_rows, a.shape[2])
            return jnp.pad(a, ((0, 0), (0, MLSTM_CHUNK - chunk_rows), (0, 0)))

        h, c_new, n_new, m_new = _mlstm(pad(qk), pad(v), pad(og), pad(gates), bias_row, g_head,
                                        c0.astype(F32), n0.astype(F32), m0p, chunk_rows)
        h = h[:, :chunk_rows].reshape(1, nb * chunk_rows, h.shape[2])
    x = _mm_norm_res(h, w_out, layer, x, mods[2], g_norm[1])
    return x, c_new, n_new, m_new[:, 0, :MH]


def kernel(x_prompt, x_sample, state_C, state_n, state_m, cache_k, cache_v, page_table, c_prompt, c_sample, w_ada, b_ada, g_norm, w_in_a, b_gate_a, g_head_a, w_out_a, w_in_b, b_sb, w_out_b, w_ffn_in, w_ffn_out):
    bp, seq, D = x_prompt.shape
    bs, dec_seq, _ = x_sample.shape
    depth = w_ada.shape[0]
    dh = D // SB_HEADS

    rows = -(-(bp + bs) // 8) * 8
    c_all = jnp.zeros((rows, D), F32).at[:bp].set(c_prompt).at[bp:bp + bs].set(c_sample)
    ada = _ada(c_all, w_ada, b_ada)

    xp = x_prompt
    xs = x_sample.reshape(1, bs * dec_seq, D)
    outs = {name: [] for name in ("kp", "vp", "ks", "vs", "Cp", "np", "mp", "Cs", "ns", "ms")}
    for i in range(depth):
        j = i // 2
        mods_p = [ada[i, :bp, s * D:(s + 1) * D].reshape(bp, 1, D) for s in range(6)]
        mods_s = [jnp.repeat(ada[i, bp:bp + bs, s * D:(s + 1) * D], dec_seq, axis=0).reshape(1, bs * dec_seq, D)
                  for s in range(6)]
        if i % 2 == 0:
            dqk = (w_in_a.shape[2] - 2 * MH) // 3 // (2 * MH)
            dv = 2 * dqk
            zc = jnp.zeros((bp, MH, dqk, dv), F32)
            zn = jnp.zeros((bp, MH, dqk), F32)
            zm = jnp.zeros((bp, MH), F32)
            xp, Cp, n_p, m_p = _mlstm_layer(xp, mods_p, g_norm[i], w_in_a, w_out_a, j, b_gate_a[j], g_head_a[j],
                                            zc, zn, zm, MLSTM_CHUNK)
            xs, Cs, n_s, m_s = _mlstm_layer(xs, mods_s, g_norm[i], w_in_a, w_out_a, j, b_gate_a[j], g_head_a[j],
                                            state_C[j], state_n[j], state_m[j], dec_seq)
            outs["Cp"].append(Cp); outs["np"].append(n_p); outs["mp"].append(m_p)
            outs["Cs"].append(Cs); outs["ns"].append(n_s); outs["ms"].append(m_s)
        else:
            tn = PROJ_TILE
            bias = b_sb[j].astype(F32)
            qp, kp, vp = _modmm(xp, g_norm[i, 0], mods_p[0], mods_p[1], w_in_b, j, (0, D, 2 * D), tn,
                                D // tn, (BF16, F32, F32), [_as_bf16, _as_f32, _as_f32])
            op = _sb_prompt(qp, kp, vp, jnp.repeat(bias, dh).reshape(1, D))
            xp = _mm_norm_res(op, w_out_b, j, xp, mods_p[2], g_norm[i, 1])
            qs, kn, vn = _modmm(xs, g_norm[i, 0], mods_s[0], mods_s[1], w_in_b, j, (0, D, 2 * D), tn,
                                D // tn, (F32, F32, F32), [_as_f32, _as_f32, _as_f32])

            def as_page(a):
                a = a.reshape(bs, dec_seq, SB_HEADS, dh)
                return jnp.pad(a, ((0, 0), (0, PAGE_SIZE - dec_seq), (0, 0), (0, 0)))

            os_ = _sb_decode(page_table, qs.reshape(bs, dec_seq, D),
                             jnp.broadcast_to(jnp.repeat(bias, dec_seq)[:, None], (SB_HEADS * dec_seq, SB_BLOCK)),
                             as_page(kn), as_page(vn), cache_k, cache_v, j)
            os_ = os_.reshape(bs, SB_HEADS, dec_seq, dh).transpose(0, 2, 1, 3).reshape(1, bs * dec_seq, D)
            xs = _mm_norm_res(os_.astype(BF16), w_out_b, j, xs, mods_s[2], g_norm[i, 1])
            outs["kp"].append(kp.reshape(bp, seq, SB_HEADS, dh)); outs["vp"].append(vp.reshape(bp, seq, SB_HEADS, dh))
            outs["ks"].append(kn.reshape(bs, dec_seq, SB_HEADS, dh)); outs["vs"].append(vn.reshape(bs, dec_seq, SB_HEADS, dh))
        xp = _ffn(xp, mods_p, g_norm[i, 2], g_norm[i, 3], w_ffn_in, w_ffn_out, i)
        xs = _ffn(xs, mods_s, g_norm[i, 2], g_norm[i, 3], w_ffn_in, w_ffn_out, i)

    st = lambda name: jnp.stack(outs[name])
    return (xp, xs.reshape(bs, dec_seq, D), st("kp"), st("vp"), st("ks"), st("vs"),
            st("Cp"), st("np"), st("mp"), st("Cs"), st("ns"), st("ms"))
```

```python
import functools

import jax
import jax.numpy as jnp
from jax import lax
from jax.experimental import pallas as pl
from jax.experimental.pallas import tpu as pltpu

F32 = jnp.float32
BF16 = jnp.bfloat16

EPS = 1e-6
MH = 8
MLSTM_CHUNK = 128
SB_HEADS = 16
SB_BLOCK = 128
SB_QTILE = 512
PAGE_SIZE = 128
NEG_BIG = -1e30

LANES = 128
SUBLANES = 8
VMEM_LIMIT = 52 * 1024 * 1024
ACC_COLS = 512
PROJ_TILE = 512
DECODE_PAGES = 4
W_BLOCK_BYTES = 6 * 1024 * 1024


def _dot(a, b):
    return jnp.dot(a, b, preferred_element_type=F32)


def _dot_nt(a, b):
    return lax.dot_general(a, b, (((1,), (1,)), ((), ())), preferred_element_type=F32)


def _dot_tn(a, b):
    return lax.dot_general(a, b, (((0,), (0,)), ((), ())), preferred_element_type=F32)


def _params(sem):
    return pltpu.CompilerParams(dimension_semantics=sem, vmem_limit_bytes=VMEM_LIMIT)


def _ada_kernel(c_ref, w_ref, b_ref, o_ref):
    c = c_ref[...]
    a = (c * jax.nn.sigmoid(c)).astype(BF16)
    o_ref[0] = _dot(a, w_ref[0].astype(BF16)) + b_ref[0]


def _ada(c_all, w_ada, b_ada, tn=1024):
    depth, d, n = w_ada.shape
    rows = c_all.shape[0]
    return pl.pallas_call(
        _ada_kernel,
        grid=(depth, n // tn),
        in_specs=[
            pl.BlockSpec((rows, d), lambda l, j: (0, 0)),
            pl.BlockSpec((1, d, tn), lambda l, j: (l, 0, j)),
            pl.BlockSpec((1, 1, tn), lambda l, j: (l, 0, j)),
        ],
        out_specs=pl.BlockSpec((1, rows, tn), lambda l, j: (l, 0, j)),
        out_shape=jax.ShapeDtypeStruct((depth, rows, n), F32),
        compiler_params=_params(("parallel", "parallel")),
        name="ada_params",
    )(c_all, w_ada, b_ada.reshape(depth, 1, n))


def _row_chunks(tl):
    rc = min(tl, 256)
    return rc, tl // rc


def _modulated(x, g, scale, shift):
    r = lax.rsqrt(jnp.mean(x * x, axis=-1, keepdims=True) + EPS)
    return ((x * r) * g) * (1.0 + scale) + shift


def _modmm_kernel(*refs, n_slices, n_tiles, epilogue, sequential, has_gate, tl, per_row):
    x_ref, g_ref, sh_ref, sc_ref = refs[:4]
    n_w = 1 if sequential else n_slices
    w_refs = refs[4:4 + n_w]
    pos = 4 + n_w
    if has_gate:
        wg_ref = refs[pos]
        pos += 1
    n_out = len(refs) - pos - 1 - (1 if has_gate else 0)
    out_refs = refs[pos:pos + n_out]
    pos += n_out
    if has_gate:
        go_ref = refs[pos]
        pos += 1
    h_scr = refs[pos]
    j = pl.program_id(2)

    @pl.when(j == 0)
    def _():
        rc, n_chunks = _row_chunks(tl)

        def chunk(c):
            rows = pl.ds(pl.multiple_of(c * rc, rc), rc)
            x = x_ref[0, rows, :]
            if per_row:
                sc, sh = sc_ref[0, rows, :], sh_ref[0, rows, :]
            else:
                sc, sh = sc_ref[0], sh_ref[0]
            h_scr[rows, :] = _modulated(x, g_ref[0], sc, sh).astype(BF16)

        if n_chunks == 1:
            chunk(0)
        else:
            pl.loop(0, n_chunks)(chunk)
        if has_gate:
            go_ref[0] = _dot(h_scr[...], wg_ref[...].astype(BF16))

    if sequential:
        for s, out_ref in enumerate(out_refs):
            @pl.when(j // n_tiles == s)
            def _(s=s, out_ref=out_ref):
                y = _dot(h_scr[...], w_refs[0][...].astype(BF16))
                out_ref[0] = epilogue[s](y, j - s * n_tiles)
    else:
        hb = h_scr[...]
        ys = [_dot(hb, w_ref[...].astype(BF16)) for w_ref in w_refs]
        epilogue(ys, out_refs, j)


def _modmm(x, g, shift, scale, w, layer, slice_starts, tn, n_tiles, out_dtypes, epilogue,
           w_gate=None, tl=1024):
    nb, L, D = x.shape
    tl = min(tl, L)
    per_row = shift.shape[1] != 1
    r = tl if per_row else 1
    n_slices = len(slice_starts)
    has_gate = w_gate is not None
    sequential = isinstance(epilogue, (list, tuple))
    if sequential:
        assert all(s0 == slice_starts[0] + s * n_tiles * tn for s, s0 in enumerate(slice_starts))

    def mod_map(b, i, j):
        return (b, i if per_row else 0, 0)

    in_specs = [
        pl.BlockSpec((1, tl, D), lambda b, i, j: (b, i, 0)),
        pl.BlockSpec((1, 1, D), lambda b, i, j: (0, 0, 0)),
        pl.BlockSpec((1, r, D), mod_map),
        pl.BlockSpec((1, r, D), mod_map),
    ]
    args = [x, g.reshape(1, 1, D), shift, scale]
    for s0 in slice_starts[:1] if sequential else slice_starts:
        off = s0 // tn
        in_specs.append(pl.BlockSpec((None, D, tn), lambda b, i, j, off=off: (layer, 0, off + j)))
        args.append(w)
    if has_gate:
        gw = w_gate.shape[1]
        in_specs.append(pl.BlockSpec((D, gw), lambda b, i, j: (0, 0)))
        args.append(w_gate)
    if sequential:
        out_specs = [
            pl.BlockSpec((1, tl, tn), lambda b, i, j, s=s: (b, i, jnp.clip(j - s * n_tiles, 0, n_tiles - 1)))
            for s in range(n_slices)]
    else:
        out_specs = [pl.BlockSpec((1, tl, tn), lambda b, i, j: (b, i, j)) for _ in out_dtypes]
    out_shape = [jax.ShapeDtypeStruct((nb, L, n_tiles * tn), dt) for dt in out_dtypes]
    if has_gate:
        out_specs.append(pl.BlockSpec((1, tl, gw), lambda b, i, j: (b, i, 0)))
        out_shape.append(jax.ShapeDtypeStruct((nb, L, gw), F32))
    return pl.pallas_call(
        functools.partial(_modmm_kernel, n_slices=n_slices, n_tiles=n_tiles, epilogue=epilogue,
                          sequential=sequential, has_gate=has_gate, tl=tl, per_row=per_row),
        grid=(nb, L // tl, n_tiles * n_slices if sequential else n_tiles),
        in_specs=in_specs,
        out_specs=out_specs,
        out_shape=out_shape,
        scratch_shapes=[pltpu.VMEM((tl, D), BF16)],
        compiler_params=_params(("parallel", "parallel", "arbitrary")),
        name="modulated_matmul",
    )(*args)


def _as_bf16(y, tile):
    return y.astype(BF16)


def _as_f32(y, tile):
    return y


def _scaled_q_then_k(y, tile, *, q_tiles, q_scale):
    return (y * jnp.where(tile < q_tiles, q_scale, 1.0).astype(F32)).astype(BF16)


def _swiglu_epilogue(ys, out_refs, j):
    gate, up = ys
    out_refs[0][0] = ((gate * jax.nn.sigmoid(gate)) * up).astype(BF16)


def _mm_norm_res_kernel(a_ref, w_ref, x_ref, gate_ref, g_ref, o_ref, *, tl, nk, per_row):
    k = pl.program_id(2)

    @pl.when(k == 0)
    def _():
        o_ref[...] = jnp.zeros_like(o_ref)

    a = a_ref[0]
    cw = min(ACC_COLS, o_ref.shape[2])
    for c in range(o_ref.shape[2] // cw):
        cols = slice(c * cw, (c + 1) * cw)
        o_ref[0, :, cols] += _dot(a, w_ref[:, cols].astype(BF16))

    @pl.when(k == nk - 1)
    def _():
        rc, n_chunks = _row_chunks(tl)

        def chunk(c):
            rows = pl.ds(pl.multiple_of(c * rc, rc), rc)
            acc = o_ref[0, rows, :]
            r = lax.rsqrt(jnp.mean(acc * acc, axis=-1, keepdims=True) + EPS)
            gate = gate_ref[0, rows, :] if per_row else gate_ref[0]
            o_ref[0, rows, :] = x_ref[0, rows, :] + gate * ((acc * r) * g_ref[0])

        if n_chunks == 1:
            chunk(0)
        else:
            pl.loop(0, n_chunks)(chunk)


def _k_tile(K, D, itemsize):
    limit = max(W_BLOCK_BYTES // (D * itemsize), LANES)
    return max(t for t in range(LANES, K + 1, LANES) if K % t == 0 and t <= limit)


def _mm_norm_res(a, w, layer, x, gate, g, tl=1024):
    nb, L, K = a.shape
    D = w.shape[2]
    tl = min(tl, L)
    per_row = gate.shape[1] != 1
    r = tl if per_row else 1
    tk = _k_tile(K, D, w.dtype.itemsize)
    nk = K // tk
    return pl.pallas_call(
        functools.partial(_mm_norm_res_kernel, tl=tl, nk=nk, per_row=per_row),
        grid=(nb, L // tl, nk),
        in_specs=[
            pl.BlockSpec((1, tl, tk), lambda b, i, k: (b, i, k)),
            pl.BlockSpec((None, tk, D), lambda b, i, k: (layer, k, 0)),
            pl.BlockSpec((1, tl, D), lambda b, i, k: (b, i, 0)),
            pl.BlockSpec((1, r, D), lambda b, i, k: (b, i if per_row else 0, 0)),
            pl.BlockSpec((1, 1, D), lambda b, i, k: (0, 0, 0)),
        ],
        out_specs=pl.BlockSpec((1, tl, D), lambda b, i, k: (b, i, 0)),
        out_shape=jax.ShapeDtypeStruct((nb, L, D), F32),
        compiler_params=_params(("parallel", "parallel", "arbitrary")),
        name="matmul_norm_residual",
    )(a, w, x, gate, g.reshape(1, 1, D))


def _scan_rows(x, op, row):
    d = 1
    while d < x.shape[0]:
        shifted = pltpu.roll(x, d, axis=0)
        x = jnp.where(row >= d, op(x, shifted), x)
        d *= 2
    return x


def _mlstm_kernel(qk_ref, v_ref, og_ref, gt_ref, bias_ref, gh_ref, c0_ref, n0_ref, m0_ref,
                  o_ref, c_out_ref, n_out_ref, m_out_ref, c_scr, n_scr, m_scr,
                  *, valid_len, nc, dqk, dv):
    c = pl.program_id(1)
    L = MLSTM_CHUNK

    @pl.when(c == 0)
    def _():
        c_scr[...] = c0_ref[0]
        n_scr[...] = n0_ref[0]
        m_scr[...] = m0_ref[0]

    gates = gt_ref[0] + bias_ref[...]
    li = gates[:, :LANES]
    gf = gates[:, LANES:]
    lf = jnp.minimum(gf, 0.0) - jnp.log1p(jnp.exp(-jnp.abs(gf)))
    row = lax.broadcasted_iota(jnp.int32, (L, LANES), 0)
    col = lax.broadcasted_iota(jnp.int32, (L, LANES), 1)
    if valid_len < L:
        li = jnp.where(row < valid_len, li, NEG_BIG)
        lf = jnp.where(row < valid_len, lf, 0.0)

    b = _scan_rows(lf, jnp.add, row)
    u = li - b
    m_prev = m_scr[...]
    m_t = b + jnp.maximum(m_prev, _scan_rows(u, jnp.maximum, row))
    d_inter = jnp.exp(b + m_prev - m_t)
    ct = b - m_t
    inv_floor = jnp.exp(-m_t)
    b_end = b[L - 1:L, :]
    m_end = m_t[L - 1:L, :]
    w_all = jnp.exp(b_end - b + li - m_end)
    decay = jnp.exp(b_end + m_prev - m_end)
    u_t = u.T
    causal = row >= col

    for h in range(MH):
        q = qk_ref[0, :, h * dqk:(h + 1) * dqk]
        k = qk_ref[0, :, MH * dqk + h * dqk:MH * dqk + (h + 1) * dqk]
        v = v_ref[0, :, h * dv:(h + 1) * dv]
        log_d = jnp.where(causal, ct[:, h:h + 1] + u_t[h:h + 1, :], NEG_BIG)
        s = _dot_nt(q, k) * jnp.exp(log_d)
        di = d_inter[:, h:h + 1]
        c_prev = c_scr[h]
        num = _dot(s.astype(BF16), v) + di * _dot(q, c_prev.astype(BF16))
        n_prev = n_scr[h:h + 1, :]
        qn_state = jnp.sum(q.astype(F32) * n_prev.astype(BF16).astype(F32), axis=-1, keepdims=True)
        qn = jnp.sum(s, axis=-1, keepdims=True) + di * qn_state
        hh = num / jnp.maximum(jnp.abs(qn), inv_floor[:, h:h + 1])
        hh = hh * lax.rsqrt(jnp.mean(hh * hh, axis=-1, keepdims=True) + EPS)
        hh = hh * gh_ref[:, h * dv:(h + 1) * dv]
        og = og_ref[0, :, h * dv:(h + 1) * dv]
        o_ref[0, :, h * dv:(h + 1) * dv] = (hh * jax.nn.sigmoid(og)).astype(BF16)

        kw = k.astype(F32) * w_all[:, h:h + 1]
        dec = decay[:, h:h + 1]
        c_scr[h] = dec * c_prev + _dot_tn(kw.astype(BF16), v)
        n_scr[h:h + 1, :] = dec * n_prev + jnp.sum(kw, axis=0, keepdims=True)

    m_scr[...] = m_end

    @pl.when(c == nc - 1)
    def _():
        c_out_ref[0] = c_scr[...]
        n_out_ref[0] = n_scr[...]
        m_out_ref[0] = m_scr[...]


def _mlstm(qk, v, og, gates, bias_row, g_head, c0, n0, m0, valid_len):
    nb, L, vt = v.shape
    dv = vt // MH
    dqk = qk.shape[2] // (2 * MH)
    nc = L // MLSTM_CHUNK
    blk = lambda width: pl.BlockSpec((1, MLSTM_CHUNK, width), lambda b, c: (b, c, 0))
    return pl.pallas_call(
        functools.partial(_mlstm_kernel, valid_len=valid_len, nc=nc, dqk=dqk, dv=dv),
        grid=(nb, nc),
        in_specs=[
            blk(qk.shape[2]), blk(vt), blk(vt), blk(2 * LANES),
            pl.BlockSpec((1, 2 * LANES), lambda b, c: (0, 0)),
            pl.BlockSpec((1, vt), lambda b, c: (0, 0)),
            pl.BlockSpec((1, MH, dqk, dv), lambda b, c: (b, 0, 0, 0)),
            pl.BlockSpec((1, MH, dqk), lambda b, c: (b, 0, 0)),
            pl.BlockSpec((1, 1, LANES), lambda b, c: (b, 0, 0)),
        ],
        out_specs=[
            blk(vt),
            pl.BlockSpec((1, MH, dqk, dv), lambda b, c: (b, 0, 0, 0)),
            pl.BlockSpec((1, MH, dqk), lambda b, c: (b, 0, 0)),
            pl.BlockSpec((1, 1, LANES), lambda b, c: (b, 0, 0)),
        ],
        out_shape=[
            jax.ShapeDtypeStruct((nb, L, vt), BF16),
            jax.ShapeDtypeStruct((nb, MH, dqk, dv), F32),
            jax.ShapeDtypeStruct((nb, MH, dqk), F32),
            jax.ShapeDtypeStruct((nb, 1, LANES), F32),
        ],
        scratch_shapes=[
            pltpu.VMEM((MH, dqk, dv), F32),
            pltpu.VMEM((MH, dqk), F32),
            pltpu.VMEM((1, LANES), F32),
        ],
        compiler_params=_params(("parallel", "arbitrary")),
        name="mlstm_chunks",
    )(qk, v, og, gates, bias_row, g_head.reshape(1, vt), c0, n0, m0)


def _suffix_matrix():
    j = lax.broadcasted_iota(jnp.int32, (2 * SB_BLOCK, 2 * SB_BLOCK), 0) % SB_BLOCK
    s = lax.broadcasted_iota(jnp.int32, (2 * SB_BLOCK, 2 * SB_BLOCK), 1)
    return jnp.where((s >= SB_BLOCK) | (j > s), 1.0, 0.0).astype(BF16)


def _sb_scores(z, mask, suffix_mat):
    soft = jnp.log(1.0 + jnp.exp(-jnp.abs(z)))
    log_beta = jnp.minimum(z, 0.0) - soft
    log_1mb = log_beta - z
    if mask is not None:
        log_1mb = jnp.where(mask, log_1mb, 0.0)
    hi = log_1mb.astype(BF16)
    lo = (log_1mb - hi.astype(F32)).astype(BF16)
    sums = _dot(jnp.concatenate([hi, lo], axis=1), suffix_mat)
    return log_beta, sums[:, :SB_BLOCK], sums[:, SB_BLOCK:]


def _sb_visit(zs, masks, value_fns, carry, suffix_mat):
    scores = [_sb_scores(z, m, suffix_mat) for z, m in zip(zs, masks)]
    pv = None
    for (log_beta, later, total), mask, value_fn in zip(scores, masks, value_fns):
        a = jnp.exp(log_beta + later + carry)
        if mask is not None:
            a = jnp.where(mask, a, 0.0)
        out = value_fn(a.astype(BF16))
        pv = out if pv is None else pv + out
        carry = carry + total
    return pv, carry


def _sb_prompt_kernel(q_ref, k_ref, v_ref, bias_ref, o_ref, k_scr, v_scr, acc_scr, carry_scr,
                      *, n_qt, scale):
    k_scr[...] = k_ref[0].astype(BF16)
    v_scr[...] = v_ref[0].astype(BF16)
    suffix_mat = _suffix_matrix()
    bias = bias_ref[...]
    sub = SB_QTILE // SB_BLOCK

    def visit(r0, r1, blocks, mask):
        q = q_ref[0, r0:r1, :]
        zs, value_fns = [], []
        for j in blocks:
            keys = pl.ds(pl.multiple_of(j * SB_BLOCK, SB_BLOCK), SB_BLOCK)
            zs.append(_dot_nt(q, k_scr[keys, :]) * scale + bias)
            value_fns.append(lambda a, keys=keys: _dot(a, v_scr[keys, :]))
        pv, carry = _sb_visit(zs, [mask] * len(zs), value_fns, carry_scr[r0:r1, :], suffix_mat)
        acc_scr[r0:r1, :] += pv
        carry_scr[r0:r1, :] = carry

    acc_scr[...] = jnp.zeros_like(acc_scr)
    carry_scr[...] = jnp.zeros_like(carry_scr)
    for c in reversed(range(sub)):
        rows_left = SB_QTILE - c * SB_BLOCK
        row = lax.broadcasted_iota(jnp.int32, (rows_left, SB_BLOCK), 0)
        col = lax.broadcasted_iota(jnp.int32, (rows_left, SB_BLOCK), 1)
        for t in range(n_qt):
            visit(t * SB_QTILE + c * SB_BLOCK, (t + 1) * SB_QTILE, [t * sub + c], col < row)

    for t in range(1, n_qt):
        def older(i, _, t=t):
            newest = (t - i) * sub - 1
            visit(t * SB_QTILE, (t + 1) * SB_QTILE, [newest - d for d in range(sub)], None)
            return 0

        lax.fori_loop(0, t, older, 0)
    o_ref[0] = acc_scr[...].astype(BF16)


def _sb_prompt(q, k, v, bias_lanes):
    nb, L, hd = q.shape
    dh = hd // SB_HEADS
    spec = pl.BlockSpec((1, L, dh), lambda b, h: (b, 0, h))
    return pl.pallas_call(
        functools.partial(_sb_prompt_kernel, n_qt=L // SB_QTILE, scale=dh ** -0.5),
        grid=(nb, SB_HEADS),
        in_specs=[spec, spec, spec, pl.BlockSpec((1, dh), lambda b, h: (0, h))],
        out_specs=spec,
        out_shape=jax.ShapeDtypeStruct((nb, L, hd), BF16),
        scratch_shapes=[pltpu.VMEM((L, dh), BF16), pltpu.VMEM((L, dh), BF16),
                        pltpu.VMEM((L, dh), F32), pltpu.VMEM((L, SB_BLOCK), F32)],
        compiler_params=_params(("parallel", "parallel")),
        name="stickbreak_prompt",
    )(q, k, v, bias_lanes)


def _sb_decode_kernel(pt_ref, q_ref, bias_ref, kn_ref, vn_ref, *rest, n_steps, n_new, scale):
    page_refs = rest[:2 * DECODE_PAGES]
    o_ref, carry_scr, acc_scr, qbd_scr, expand_scr, hmask_scr = rest[2 * DECODE_PAGES:]
    p = pl.program_id(1)
    rows = SB_HEADS * n_new
    dh = q_ref.shape[2] // SB_HEADS
    suffix_mat = _suffix_matrix()

    @pl.when(p == 0)
    def _():
        carry_scr[...] = jnp.zeros_like(carry_scr)
        acc_scr[...] = jnp.zeros_like(acc_scr)
        q_rep = jnp.concatenate([q_ref[0]] * SB_HEADS, axis=0)
        lane_head = lax.broadcasted_iota(jnp.int32, q_rep.shape, 1) // dh
        row_head = lax.broadcasted_iota(jnp.int32, q_rep.shape, 0) // n_new
        qbd_scr[...] = jnp.where(lane_head == row_head, q_rep, 0.0).astype(BF16)
        tok = lax.broadcasted_iota(jnp.int32, expand_scr.shape, 0)
        c = lax.broadcasted_iota(jnp.int32, expand_scr.shape, 1)
        expand_scr[...] = jnp.where(c // SB_HEADS == tok, 1.0, 0.0).astype(BF16)
        r = lax.broadcasted_iota(jnp.int32, hmask_scr.shape, 0)
        c = lax.broadcasted_iota(jnp.int32, hmask_scr.shape, 1)
        hmask_scr[...] = jnp.where(c % SB_HEADS == r // n_new, 1.0, 0.0).astype(BF16)

    def attend(kv_refs, mask):
        zs, value_fns = [], []
        for k_ref, v_ref in kv_refs:
            k_all = jnp.concatenate(
                [k_ref[pl.ds(h, PAGE_SIZE, stride=SB_HEADS), :].astype(BF16) for h in range(SB_HEADS)],
                axis=1)
            zs.append(_dot_nt(qbd_scr[...], k_all) * scale + bias_ref[...])

            def values(a, v_ref=v_ref):
                spread = _dot(a, expand_scr[...]).astype(BF16) * hmask_scr[...]
                return _dot(spread, v_ref[...].astype(BF16))

            value_fns.append(values)
        pv, carry = _sb_visit(zs, [mask] * len(zs), value_fns, carry_scr[...], suffix_mat)
        acc_scr[...] += pv
        carry_scr[...] = carry

    @pl.when(p == 0)
    def _():
        row = lax.broadcasted_iota(jnp.int32, (rows, SB_BLOCK), 0)
        col = lax.broadcasted_iota(jnp.int32, (rows, SB_BLOCK), 1)
        attend([(kn_ref, vn_ref)], col < row % n_new)

    @pl.when(p > 0)
    def _():
        attend([(page_refs[2 * i], page_refs[2 * i + 1]) for i in range(DECODE_PAGES)], None)

    @pl.when(p == n_steps - 1)
    def _():
        o_ref[0] = acc_scr[...]


def _sb_decode(page_table, q, bias_rows, k_new, v_new, pool_k, pool_v, layer):
    nb, n_new, hd = q.shape
    dh = hd // SB_HEADS
    n_pages = page_table.shape[1]
    rows = SB_HEADS * n_new
    prow = PAGE_SIZE * SB_HEADS
    assert n_pages % DECODE_PAGES == 0 and rows == SB_BLOCK and PAGE_SIZE == SB_BLOCK
    n_steps = n_pages // DECODE_PAGES + 1
    n_layers, n_pool = pool_v.shape[:2]
    pool_k = pool_k.reshape(n_layers, n_pool, prow, dh)
    pool_v = pool_v.reshape(n_layers, n_pool, prow, dh)
    k_new = k_new.reshape(nb, prow, dh)
    v_new = v_new.reshape(nb, prow, dh)

    def page_spec(slot):
        return pl.BlockSpec(
            (None, None, prow, dh),
            lambda b, p, pt: (layer, pt[b, n_pages - DECODE_PAGES * jnp.maximum(p, 1) + DECODE_PAGES - 1 - slot], 0, 0))

    new_spec = pl.BlockSpec((None, prow, dh), lambda b, p, pt: (b, 0, 0))
    grid_spec = pltpu.PrefetchScalarGridSpec(
        num_scalar_prefetch=1,
        grid=(nb, n_steps),
        in_specs=[
            pl.BlockSpec((1, n_new, hd), lambda b, p, pt: (b, 0, 0)),
            pl.BlockSpec((rows, SB_BLOCK), lambda b, p, pt: (0, 0)),
            new_spec, new_spec,
        ] + [page_spec(slot) for slot in range(DECODE_PAGES) for _ in range(2)],
        out_specs=pl.BlockSpec((1, rows, dh), lambda b, p, pt: (b, 0, 0)),
        scratch_shapes=[pltpu.VMEM((rows, SB_BLOCK), F32), pltpu.VMEM((rows, dh), F32),
                        pltpu.VMEM((rows, hd), BF16), pltpu.VMEM((PAGE_SIZE, prow), BF16),
                        pltpu.VMEM((rows, prow), BF16)],
    )
    return pl.pallas_call(
        functools.partial(_sb_decode_kernel, n_steps=n_steps, n_new=n_new, scale=dh ** -0.5),
        grid_spec=grid_spec,
        out_shape=jax.ShapeDtypeStruct((nb, rows, dh), F32),
        compiler_params=_params(("parallel", "arbitrary")),
        name="stickbreak_decode",
    )(page_table, q, bias_rows, k_new, v_new, *([pool_k, pool_v] * DECODE_PAGES))


def _ffn(x, mods, g_pre, g_post, w_in, w_out, layer, tf=512):
    d_ff = w_out.shape[1]
    (act,) = _modmm(x, g_pre, mods[3], mods[4], w_in, layer, (0, d_ff), tf, d_ff // tf,
                    (BF16,), _swiglu_epilogue)
    return _mm_norm_res(act, w_out, layer, x, mods[5], g_post)


def _mlstm_layer(x, mods, g_norm, w_in, w_out, layer, b_gate, g_head, c0, n0, m0, chunk_rows):
    D = x.shape[2]
    qk_tot = (w_in.shape[2] - 2 * MH) // 3
    dqk = qk_tot // (2 * MH)
    tn = PROJ_TILE
    w_gate = jnp.zeros((D, 2 * LANES), F32)
    w_gate = w_gate.at[:, :MH].set(w_in[layer, :, 3 * qk_tot:3 * qk_tot + MH])
    w_gate = w_gate.at[:, LANES:LANES + MH].set(w_in[layer, :, 3 * qk_tot + MH:])
    bias_row = jnp.zeros((1, 2 * LANES), F32)
    bias_row = bias_row.at[0, :MH].set(b_gate[0].astype(F32))
    bias_row = bias_row.at[0, LANES:LANES + MH].set(b_gate[1].astype(F32))
    epilogue = [functools.partial(_scaled_q_then_k, q_tiles=(MH * dqk) // tn, q_scale=dqk ** -0.5),
                _as_bf16, _as_f32]
    qk, v, og, gates = _modmm(x, g_norm[0], mods[0], mods[1], w_in, layer, (0, qk_tot, 2 * qk_tot), tn,
                              qk_tot // tn, (BF16, BF16, F32), epilogue, w_gate=w_gate)
    m0p = jnp.zeros((m0.shape[0], 1, LANES), F32).at[:, 0, :MH].set(m0.astype(F32))
    if chunk_rows == MLSTM_CHUNK:
        h, c_new, n_new, m_new = _mlstm(qk, v, og, gates, bias_row, g_head,
                                        c0.astype(F32), n0.astype(F32), m0p, MLSTM_CHUNK)
    else:
        nb = x.shape[1] // chunk_rows

        def pad(a):
            a = a.reshape(nb, chunk_rows, a.shape[2])
            return jnp.pad(a, ((0, 0), (0, MLSTM_CHUNK - chunk_rows), (0, 0)))

        h, c_new, n_new, m_new = _mlstm(pad(qk), pad(v), pad(og), pad(gates), bias_row, g_head,
                                        c0.astype(F32), n0.astype(F32), m0p, chunk_rows)
        h = h[:, :chunk_rows].reshape(1, nb * chunk_rows, h.shape[2])
    x = _mm_norm_res(h, w_out, layer, x, mods[2], g_norm[1])
    return x, c_new, n_new, m_new[:, 0, :MH]


def kernel(x_prompt, x_sample, state_C, state_n, state_m, cache_k, cache_v, page_table, c_prompt, c_sample, w_ada, b_ada, g_norm, w_in_a, b_gate_a, g_head_a, w_out_a, w_in_b, b_sb, w_out_b, w_ffn_in, w_ffn_out):
    bp, seq, D = x_prompt.shape
    bs, dec_seq, _ = x_sample.shape
    depth = w_ada.shape[0]
    dh = D // SB_HEADS

    rows = -(-(bp + bs) // 8) * 8
    c_all = jnp.zeros((rows, D), F32).at[:bp].set(c_prompt).at[bp:bp + bs].set(c_sample)
    ada = _ada(c_all, w_ada, b_ada)

    w_in_a, w_out_a, w_in_b, w_out_b, w_ffn_out = (
        w.astype(BF16) for w in (w_in_a, w_out_a, w_in_b, w_out_b, w_ffn_out))

    xp = x_prompt
    xs = x_sample.reshape(1, bs * dec_seq, D)
    outs = {name: [] for name in ("kp", "vp", "ks", "vs", "Cp", "np", "mp", "Cs", "ns", "ms")}
    for i in range(depth):
        j = i // 2
        mods_p = [ada[i, :bp, s * D:(s + 1) * D].reshape(bp, 1, D) for s in range(6)]
        mods_s = [jnp.repeat(ada[i, bp:bp + bs, s * D:(s + 1) * D], dec_seq, axis=0).reshape(1, bs * dec_seq, D)
                  for s in range(6)]
        if i % 2 == 0:
            dqk = (w_in_a.shape[2] - 2 * MH) // 3 // (2 * MH)
            dv = 2 * dqk
            zc = jnp.zeros((bp, MH, dqk, dv), F32)
            zn = jnp.zeros((bp, MH, dqk), F32)
            zm = jnp.zeros((bp, MH), F32)
            xp, Cp, n_p, m_p = _mlstm_layer(xp, mods_p, g_norm[i], w_in_a, w_out_a, j, b_gate_a[j], g_head_a[j],
                                            zc, zn, zm, MLSTM_CHUNK)
            xs, Cs, n_s, m_s = _mlstm_layer(xs, mods_s, g_norm[i], w_in_a, w_out_a, j, b_gate_a[j], g_head_a[j],
                                            state_C[j], state_n[j], state_m[j], dec_seq)
            outs["Cp"].append(Cp); outs["np"].append(n_p); outs["mp"].append(m_p)
            outs["Cs"].append(Cs); outs["ns"].append(n_s); outs["ms"].append(m_s)
        else:
            tn = PROJ_TILE
            bias = b_sb[j].astype(F32)
            qp, kp, vp = _modmm(xp, g_norm[i, 0], mods_p[0], mods_p[1], w_in_b, j, (0, D, 2 * D), tn,
                                D // tn, (BF16, F32, F32), [_as_bf16, _as_f32, _as_f32])
            op = _sb_prompt(qp, kp, vp, jnp.repeat(bias, dh).reshape(1, D))
            xp = _mm_norm_res(op, w_out_b, j, xp, mods_p[2], g_norm[i, 1])
            qs, kn, vn = _modmm(xs, g_norm[i, 0], mods_s[0], mods_s[1], w_in_b, j, (0, D, 2 * D), tn,
                                D // tn, (F32, F32, F32), [_as_f32, _as_f32, _as_f32])

            def as_page(a):
                a = a.reshape(bs, dec_seq, SB_HEADS, dh)
                return jnp.pad(a, ((0, 0), (0, PAGE_SIZE - dec_seq), (0, 0), (0, 0)))

            os_ = _sb_decode(page_table, qs.reshape(bs, dec_seq, D),
                             jnp.broadcast_to(jnp.repeat(bias, dec_seq)[:, None], (SB_HEADS * dec_seq, SB_BLOCK)),
                             as_page(kn), as_page(vn), cache_k, cache_v, j)
            os_ = os_.reshape(bs, SB_HEADS, dec_seq, dh).transpose(0, 2, 1, 3).reshape(1, bs * dec_seq, D)
            xs = _mm_norm_res(os_.astype(BF16), w_out_b, j, xs, mods_s[2], g_norm[i, 1])
            outs["kp"].append(kp.reshape(bp, seq, SB_HEADS, dh)); outs["vp"].append(vp.reshape(bp, seq, SB_HEADS, dh))
            outs["ks"].append(kn.reshape(bs, dec_seq, SB_HEADS, dh)); outs["vs"].append(vn.reshape(bs, dec_seq, SB_HEADS, dh))
        xp = _ffn(xp, mods_p, g_norm[i, 2], g_norm[i, 3], w_ffn_in, w_ffn_out, i)
        xs = _ffn(xs, mods_s, g_norm[i, 2], g_norm[i, 3], w_ffn_in, w_ffn_out, i)

    st = lambda name: jnp.stack(outs[name])
    return (xp, xs.reshape(bs, dec_seq, D), st("kp"), st("vp"), st("ks"), st("vs"),
            st("Cp"), st("np"), st("mp"), st("Cs"), st("ns"), st("ms"))
```

```python
import functools

import jax
import jax.numpy as jnp
from jax import lax
from jax.experimental import pallas as pl
from jax.experimental.pallas import tpu as pltpu

F32 = jnp.float32
BF16 = jnp.bfloat16

EPS = 1e-6
MH = 8
MLSTM_CHUNK = 128
SB_HEADS = 16
SB_BLOCK = 128
SB_QTILE = 512
PAGE_SIZE = 128
NEG_BIG = -1e30

LANES = 128
SUBLANES = 8
VMEM_LIMIT = 52 * 1024 * 1024
ACC_COLS = 512
PROJ_TILE = 512
DECODE_PAGES = 8
W_BLOCK_BYTES = 6 * 1024 * 1024


def _dot(a, b):
    return jnp.dot(a, b, preferred_element_type=F32)


def _dot_nt(a, b):
    return lax.dot_general(a, b, (((1,), (1,)), ((), ())), preferred_element_type=F32)


def _dot_tn(a, b):
    return lax.dot_general(a, b, (((0,), (0,)), ((), ())), preferred_element_type=F32)


def _params(sem):
    return pltpu.CompilerParams(dimension_semantics=sem, vmem_limit_bytes=VMEM_LIMIT)


def _ada_kernel(c_ref, w_ref, b_ref, o_ref):
    c = c_ref[...]
    a = (c * jax.nn.sigmoid(c)).astype(BF16)
    o_ref[0] = _dot(a, w_ref[0].astype(BF16)) + b_ref[0]


def _ada(c_all, w_ada, b_ada, tn=1024):
    depth, d, n = w_ada.shape
    rows = c_all.shape[0]
    return pl.pallas_call(
        _ada_kernel,
        grid=(depth, n // tn),
        in_specs=[
            pl.BlockSpec((rows, d), lambda l, j: (0, 0)),
            pl.BlockSpec((1, d, tn), lambda l, j: (l, 0, j)),
            pl.BlockSpec((1, 1, tn), lambda l, j: (l, 0, j)),
        ],
        out_specs=pl.BlockSpec((1, rows, tn), lambda l, j: (l, 0, j)),
        out_shape=jax.ShapeDtypeStruct((depth, rows, n), F32),
        compiler_params=_params(("parallel", "parallel")),
        name="ada_params",
    )(c_all, w_ada, b_ada.reshape(depth, 1, n))


def _row_chunks(tl):
    rc = min(tl, 256)
    return rc, tl // rc


def _for_row_chunks(tl, body):
    rc, n_chunks = _row_chunks(tl)
    if n_chunks == 1:
        body(pl.ds(0, rc))
    else:
        pl.loop(0, n_chunks)(lambda c: body(pl.ds(pl.multiple_of(c * rc, rc), rc)))


def _store_inv_rms(read_rows, r_scr, tl):
    def body(rows):
        v = read_rows(rows)
        r = lax.rsqrt(jnp.mean(v * v, axis=-1, keepdims=True) + EPS)
        r_scr[rows, :] = jnp.broadcast_to(r, (r.shape[0], LANES))

    _for_row_chunks(tl, body)


def _modmm_kernel(*refs, n_slices, n_tiles, epilogue, sequential, has_gate, tl, per_row):
    x_ref, g_ref, sh_ref, sc_ref = refs[:4]
    n_w = 1 if sequential else n_slices
    w_refs = refs[4:4 + n_w]
    pos = 4 + n_w
    if has_gate:
        wg_ref = refs[pos]
        pos += 1
    n_out = len(refs) - pos - 2 - (1 if has_gate else 0)
    out_refs = refs[pos:pos + n_out]
    pos += n_out
    if has_gate:
        go_ref = refs[pos]
        pos += 1
    h_scr, r_scr = refs[pos], refs[pos + 1]
    j = pl.program_id(2)

    @pl.when(j == 0)
    def _():
        _store_inv_rms(lambda rows: x_ref[0, rows, :], r_scr, tl)

        def modulate(rows):
            if per_row:
                sc, sh = sc_ref[0, rows, :], sh_ref[0, rows, :]
            else:
                sc, sh = sc_ref[0], sh_ref[0]
            gain = g_ref[0] * (1.0 + sc)
            h = (x_ref[0, rows, :] * r_scr[rows, :1]) * gain + sh
            h_scr[rows, :] = h.astype(BF16)

        _for_row_chunks(tl, modulate)
        if has_gate:
            go_ref[0] = _dot(h_scr[...], wg_ref[...].astype(BF16))

    if sequential:
        for s, out_ref in enumerate(out_refs):
            @pl.when(j // n_tiles == s)
            def _(s=s, out_ref=out_ref):
                y = _dot(h_scr[...], w_refs[0][...].astype(BF16))
                out_ref[0] = epilogue[s](y, j - s * n_tiles)
    else:
        hb = h_scr[...]
        ys = [_dot(hb, w_ref[...].astype(BF16)) for w_ref in w_refs]
        epilogue(ys, out_refs, j)


def _modmm(x, g, shift, scale, w, layer, slice_starts, tn, n_tiles, out_dtypes, epilogue,
           w_gate=None, tl=1024):
    nb, L, D = x.shape
    tl = min(tl, L)
    per_row = shift.shape[1] != 1
    r = tl if per_row else 1
    n_slices = len(slice_starts)
    has_gate = w_gate is not None
    sequential = isinstance(epilogue, (list, tuple))
    if sequential:
        assert all(s0 == slice_starts[0] + s * n_tiles * tn for s, s0 in enumerate(slice_starts))

    def mod_map(b, i, j):
        return (b, i if per_row else 0, 0)

    in_specs = [
        pl.BlockSpec((1, tl, D), lambda b, i, j: (b, i, 0)),
        pl.BlockSpec((1, 1, D), lambda b, i, j: (0, 0, 0)),
        pl.BlockSpec((1, r, D), mod_map),
        pl.BlockSpec((1, r, D), mod_map),
    ]
    args = [x, g.reshape(1, 1, D), shift, scale]
    for s0 in slice_starts[:1] if sequential else slice_starts:
        off = s0 // tn
        in_specs.append(pl.BlockSpec((None, D, tn), lambda b, i, j, off=off: (layer, 0, off + j)))
        args.append(w)
    if has_gate:
        gw = w_gate.shape[1]
        in_specs.append(pl.BlockSpec((D, gw), lambda b, i, j: (0, 0)))
        args.append(w_gate)
    if sequential:
        out_specs = [
            pl.BlockSpec((1, tl, tn), lambda b, i, j, s=s: (b, i, jnp.clip(j - s * n_tiles, 0, n_tiles - 1)))
            for s in range(n_slices)]
    else:
        out_specs = [pl.BlockSpec((1, tl, tn), lambda b, i, j: (b, i, j)) for _ in out_dtypes]
    out_shape = [jax.ShapeDtypeStruct((nb, L, n_tiles * tn), dt) for dt in out_dtypes]
    if has_gate:
        out_specs.append(pl.BlockSpec((1, tl, gw), lambda b, i, j: (b, i, 0)))
        out_shape.append(jax.ShapeDtypeStruct((nb, L, gw), F32))
    return pl.pallas_call(
        functools.partial(_modmm_kernel, n_slices=n_slices, n_tiles=n_tiles, epilogue=epilogue,
                          sequential=sequential, has_gate=has_gate, tl=tl, per_row=per_row),
        grid=(nb, L // tl, n_tiles * n_slices if sequential else n_tiles),
        in_specs=in_specs,
        out_specs=out_specs,
        out_shape=out_shape,
        scratch_shapes=[pltpu.VMEM((tl, D), BF16), pltpu.VMEM((tl, LANES), F32)],
        compiler_params=_params(("parallel", "parallel", "arbitrary")),
        name="modulated_matmul",
    )(*args)


def _as_bf16(y, tile):
    return y.astype(BF16)


def _as_f32(y, tile):
    return y


def _scaled_q_then_k(y, tile, *, q_tiles, q_scale):
    return (y * jnp.where(tile < q_tiles, q_scale, 1.0).astype(F32)).astype(BF16)


def _swiglu_epilogue(ys, out_refs, j):
    gate, up = ys
    out_refs[0][0] = ((gate * jax.nn.sigmoid(gate)) * up).astype(BF16)


def _mm_norm_res_kernel(a_ref, w_ref, x_ref, gate_ref, g_ref, o_ref, r_scr, *, tl, nk, per_row):
    k = pl.program_id(2)

    @pl.when(k == 0)
    def _():
        o_ref[...] = jnp.zeros_like(o_ref)

    a = a_ref[0]
    cw = min(ACC_COLS, o_ref.shape[2])
    for c in range(o_ref.shape[2] // cw):
        cols = slice(c * cw, (c + 1) * cw)
        o_ref[0, :, cols] += _dot(a, w_ref[:, cols].astype(BF16))

    @pl.when(k == nk - 1)
    def _():
        _store_inv_rms(lambda rows: o_ref[0, rows, :], r_scr, tl)

        def finish(rows):
            gate = gate_ref[0, rows, :] if per_row else gate_ref[0]
            gain = gate * g_ref[0]
            o_ref[0, rows, :] = x_ref[0, rows, :] + (o_ref[0, rows, :] * r_scr[rows, :1]) * gain

        _for_row_chunks(tl, finish)


def _k_tile(K, D, itemsize):
    limit = max(W_BLOCK_BYTES // (D * itemsize), LANES)
    return max(t for t in range(LANES, K + 1, LANES) if K % t == 0 and t <= limit)


def _mm_norm_res(a, w, layer, x, gate, g, tl=1024):
    nb, L, K = a.shape
    D = w.shape[2]
    tl = min(tl, L)
    per_row = gate.shape[1] != 1
    r = tl if per_row else 1
    tk = _k_tile(K, D, w.dtype.itemsize)
    nk = K // tk
    return pl.pallas_call(
        functools.partial(_mm_norm_res_kernel, tl=tl, nk=nk, per_row=per_row),
        grid=(nb, L // tl, nk),
        in_specs=[
            pl.BlockSpec((1, tl, tk), lambda b, i, k: (b, i, k)),
            pl.BlockSpec((None, tk, D), lambda b, i, k: (layer, k, 0)),
            pl.BlockSpec((1, tl, D), lambda b, i, k: (b, i, 0)),
            pl.BlockSpec((1, r, D), lambda b, i, k: (b, i if per_row else 0, 0)),
            pl.BlockSpec((1, 1, D), lambda b, i, k: (0, 0, 0)),
        ],
        out_specs=pl.BlockSpec((1, tl, D), lambda b, i, k: (b, i, 0)),
        out_shape=jax.ShapeDtypeStruct((nb, L, D), F32),
        scratch_shapes=[pltpu.VMEM((tl, LANES), F32)],
        compiler_params=_params(("parallel", "parallel", "arbitrary")),
        name="matmul_norm_residual",
    )(a, w, x, gate, g.reshape(1, 1, D))


def _scan_rows(x, op, row):
    d = 1
    while d < x.shape[0]:
        shifted = pltpu.roll(x, d, axis=0)
        x = jnp.where(row >= d, op(x, shifted), x)
        d *= 2
    return x


def _mlstm_kernel(qk_ref, v_ref, og_ref, gt_ref, bias_ref, gh_ref, c0_ref, n0_ref, m0_ref,
                  o_ref, c_out_ref, n_out_ref, m_out_ref, c_scr, n_scr, m_scr,
                  *, valid_len, nc, dqk, dv):
    c = pl.program_id(1)
    L = MLSTM_CHUNK

    @pl.when(c == 0)
    def _():
        c_scr[...] = c0_ref[0]
        n_scr[...] = n0_ref[0]
        m_scr[...] = m0_ref[0]

    gates = gt_ref[0] + bias_ref[...]
    li = gates[:, :LANES]
    gf = gates[:, LANES:]
    lf = jnp.minimum(gf, 0.0) - jnp.log1p(jnp.exp(-jnp.abs(gf)))
    row = lax.broadcasted_iota(jnp.int32, (L, LANES), 0)
    col = lax.broadcasted_iota(jnp.int32, (L, LANES), 1)
    if valid_len < L:
        li = jnp.where(row < valid_len, li, NEG_BIG)
        lf = jnp.where(row < valid_len, lf, 0.0)

    b = _scan_rows(lf, jnp.add, row)
    u = li - b
    m_prev = m_scr[...]
    m_t = b + jnp.maximum(m_prev, _scan_rows(u, jnp.maximum, row))
    d_inter = jnp.exp(b + m_prev - m_t)
    ct = b - m_t
    inv_floor = jnp.exp(-m_t)
    b_end = b[L - 1:L, :]
    m_end = m_t[L - 1:L, :]
    w_all = jnp.exp(b_end - b + li - m_end)
    decay = jnp.exp(b_end + m_prev - m_end)
    u_t = u.T
    causal = row >= col

    for h in range(MH):
        q = qk_ref[0, :, h * dqk:(h + 1) * dqk]
        k = qk_ref[0, :, MH * dqk + h * dqk:MH * dqk + (h + 1) * dqk]
        v = v_ref[0, :, h * dv:(h + 1) * dv]
        log_d = jnp.where(causal, ct[:, h:h + 1] + u_t[h:h + 1, :], NEG_BIG)
        s = _dot_nt(q, k) * jnp.exp(log_d)
        di = d_inter[:, h:h + 1]
        c_prev = c_scr[h]
        num = _dot(s.astype(BF16), v) + di * _dot(q, c_prev.astype(BF16))
        n_prev = n_scr[h:h + 1, :]
        qn_state = jnp.sum(q.astype(F32) * n_prev.astype(BF16).astype(F32), axis=-1, keepdims=True)
        qn = jnp.sum(s, axis=-1, keepdims=True) + di * qn_state
        hh = num / jnp.maximum(jnp.abs(qn), inv_floor[:, h:h + 1])
        hh = hh * lax.rsqrt(jnp.mean(hh * hh, axis=-1, keepdims=True) + EPS)
        hh = hh * gh_ref[:, h * dv:(h + 1) * dv]
        og = og_ref[0, :, h * dv:(h + 1) * dv]
        o_ref[0, :, h * dv:(h + 1) * dv] = (hh * jax.nn.sigmoid(og)).astype(BF16)

        kw = k.astype(F32) * w_all[:, h:h + 1]
        dec = decay[:, h:h + 1]
        c_scr[h] = dec * c_prev + _dot_tn(kw.astype(BF16), v)
        n_scr[h:h + 1, :] = dec * n_prev + jnp.sum(kw, axis=0, keepdims=True)

    m_scr[...] = m_end

    @pl.when(c == nc - 1)
    def _():
        c_out_ref[0] = c_scr[...]
        n_out_ref[0] = n_scr[...]
        m_out_ref[0] = m_scr[...]


def _mlstm(qk, v, og, gates, bias_row, g_head, c0, n0, m0, valid_len):
    nb, L, vt = v.shape
    dv = vt // MH
    dqk = qk.shape[2] // (2 * MH)
    nc = L // MLSTM_CHUNK
    blk = lambda width: pl.BlockSpec((1, MLSTM_CHUNK, width), lambda b, c: (b, c, 0))
    return pl.pallas_call(
        functools.partial(_mlstm_kernel, valid_len=valid_len, nc=nc, dqk=dqk, dv=dv),
        grid=(nb, nc),
        in_specs=[
            blk(qk.shape[2]), blk(vt), blk(vt), blk(2 * LANES),
            pl.BlockSpec((1, 2 * LANES), lambda b, c: (0, 0)),
            pl.BlockSpec((1, vt), lambda b, c: (0, 0)),
            pl.BlockSpec((1, MH, dqk, dv), lambda b, c: (b, 0, 0, 0)),
            pl.BlockSpec((1, MH, dqk), lambda b, c: (b, 0, 0)),
            pl.BlockSpec((1, 1, LANES), lambda b, c: (b, 0, 0)),
        ],
        out_specs=[
            blk(vt),
            pl.BlockSpec((1, MH, dqk, dv), lambda b, c: (b, 0, 0, 0)),
            pl.BlockSpec((1, MH, dqk), lambda b, c: (b, 0, 0)),
            pl.BlockSpec((1, 1, LANES), lambda b, c: (b, 0, 0)),
        ],
        out_shape=[
            jax.ShapeDtypeStruct((nb, L, vt), BF16),
            jax.ShapeDtypeStruct((nb, MH, dqk, dv), F32),
            jax.ShapeDtypeStruct((nb, MH, dqk), F32),
            jax.ShapeDtypeStruct((nb, 1, LANES), F32),
        ],
        scratch_shapes=[
            pltpu.VMEM((MH, dqk, dv), F32),
            pltpu.VMEM((MH, dqk), F32),
            pltpu.VMEM((1, LANES), F32),
        ],
        compiler_params=_params(("parallel", "arbitrary")),
        name="mlstm_chunks",
    )(qk, v, og, gates, bias_row, g_head.reshape(1, vt), c0, n0, m0)


def _suffix_matrix():
    j = lax.broadcasted_iota(jnp.int32, (2 * SB_BLOCK, 2 * SB_BLOCK), 0) % SB_BLOCK
    s = lax.broadcasted_iota(jnp.int32, (2 * SB_BLOCK, 2 * SB_BLOCK), 1)
    return jnp.where((s >= SB_BLOCK) | (j > s), 1.0, 0.0).astype(BF16)


def _sb_scores(z, mask, suffix_mat):
    soft = jnp.log(1.0 + jnp.exp(-jnp.abs(z)))
    log_beta = jnp.minimum(z, 0.0) - soft
    log_1mb = log_beta - z
    if mask is not None:
        log_1mb = jnp.where(mask, log_1mb, 0.0)
    hi = log_1mb.astype(BF16)
    lo = (log_1mb - hi.astype(F32)).astype(BF16)
    sums = _dot(jnp.concatenate([hi, lo], axis=1), suffix_mat)
    return log_beta, sums[:, :SB_BLOCK], sums[:, SB_BLOCK:]


def _sb_visit(zs, masks, value_fns, carry, suffix_mat):
    scores = [_sb_scores(z, m, suffix_mat) for z, m in zip(zs, masks)]
    pv = None
    for (log_beta, later, total), mask, value_fn in zip(scores, masks, value_fns):
        a = jnp.exp(log_beta + later + carry)
        if mask is not None:
            a = jnp.where(mask, a, 0.0)
        out = value_fn(a.astype(BF16))
        pv = out if pv is None else pv + out
        carry = carry + total
    return pv, carry


def _sb_prompt_kernel(q_ref, k_ref, v_ref, bias_ref, o_ref, k_scr, v_scr, acc_scr, carry_scr,
                      *, n_qt, scale):
    k_scr[...] = k_ref[0].astype(BF16)
    v_scr[...] = v_ref[0].astype(BF16)
    suffix_mat = _suffix_matrix()
    bias = bias_ref[...]
    sub = SB_QTILE // SB_BLOCK

    def visit(r0, r1, blocks, mask):
        q = q_ref[0, r0:r1, :]
        zs, value_fns = [], []
        for j in blocks:
            keys = pl.ds(pl.multiple_of(j * SB_BLOCK, SB_BLOCK), SB_BLOCK)
            zs.append(_dot_nt(q, k_scr[keys, :]) * scale + bias)
            value_fns.append(lambda a, keys=keys: _dot(a, v_scr[keys, :]))
        pv, carry = _sb_visit(zs, [mask] * len(zs), value_fns, carry_scr[r0:r1, :], suffix_mat)
        acc_scr[r0:r1, :] += pv
        carry_scr[r0:r1, :] = carry

    acc_scr[...] = jnp.zeros_like(acc_scr)
    carry_scr[...] = jnp.zeros_like(carry_scr)
    for c in reversed(range(sub)):
        rows_left = SB_QTILE - c * SB_BLOCK
        row = lax.broadcasted_iota(jnp.int32, (rows_left, SB_BLOCK), 0)
        col = lax.broadcasted_iota(jnp.int32, (rows_left, SB_BLOCK), 1)
        for t in range(n_qt):
            visit(t * SB_QTILE + c * SB_BLOCK, (t + 1) * SB_QTILE, [t * sub + c], col < row)

    for t in range(1, n_qt):
        def older(i, _, t=t):
            newest = (t - i) * sub - 1
            visit(t * SB_QTILE, (t + 1) * SB_QTILE, [newest - d for d in range(sub)], None)
            return 0

        lax.fori_loop(0, t, older, 0)
    o_ref[0] = acc_scr[...].astype(BF16)


def _sb_prompt(q, k, v, bias_lanes):
    nb, L, hd = q.shape
    dh = hd // SB_HEADS
    spec = pl.BlockSpec((1, L, dh), lambda b, h: (b, 0, h))
    return pl.pallas_call(
        functools.partial(_sb_prompt_kernel, n_qt=L // SB_QTILE, scale=dh ** -0.5),
        grid=(nb, SB_HEADS),
        in_specs=[spec, spec, spec, pl.BlockSpec((1, dh), lambda b, h: (0, h))],
        out_specs=spec,
        out_shape=jax.ShapeDtypeStruct((nb, L, hd), BF16),
        scratch_shapes=[pltpu.VMEM((L, dh), BF16), pltpu.VMEM((L, dh), BF16),
                        pltpu.VMEM((L, dh), F32), pltpu.VMEM((L, SB_BLOCK), F32)],
        compiler_params=_params(("parallel", "parallel")),
        name="stickbreak_prompt",
    )(q, k, v, bias_lanes)


def _sb_decode_kernel(pt_ref, q_ref, bias_ref, kn_ref, vn_ref, *rest, n_steps, n_new, scale):
    page_refs = rest[:2 * DECODE_PAGES]
    o_ref, carry_scr, acc_scr, qbd_scr, expand_scr, hmask_scr = rest[2 * DECODE_PAGES:]
    p = pl.program_id(1)
    rows = SB_HEADS * n_new
    dh = q_ref.shape[2] // SB_HEADS
    suffix_mat = _suffix_matrix()

    @pl.when(p == 0)
    def _():
        carry_scr[...] = jnp.zeros_like(carry_scr)
        acc_scr[...] = jnp.zeros_like(acc_scr)
        q_rep = jnp.concatenate([q_ref[0]] * SB_HEADS, axis=0)
        lane_head = lax.broadcasted_iota(jnp.int32, q_rep.shape, 1) // dh
        row_head = lax.broadcasted_iota(jnp.int32, q_rep.shape, 0) // n_new
        qbd_scr[...] = jnp.where(lane_head == row_head, q_rep, 0.0).astype(BF16)
        tok = lax.broadcasted_iota(jnp.int32, expand_scr.shape, 0)
        c = lax.broadcasted_iota(jnp.int32, expand_scr.shape, 1)
        expand_scr[...] = jnp.where(c // SB_HEADS == tok, 1.0, 0.0).astype(BF16)
        r = lax.broadcasted_iota(jnp.int32, hmask_scr.shape, 0)
        c = lax.broadcasted_iota(jnp.int32, hmask_scr.shape, 1)
        hmask_scr[...] = jnp.where(c % SB_HEADS == r // n_new, 1.0, 0.0).astype(BF16)

    def attend(kv_refs, mask):
        zs, value_fns = [], []
        for k_ref, v_ref in kv_refs:
            k_all = jnp.concatenate(
                [k_ref[pl.ds(h, PAGE_SIZE, stride=SB_HEADS), :].astype(BF16) for h in range(SB_HEADS)],
                axis=1)
            zs.append(_dot_nt(qbd_scr[...], k_all) * scale + bias_ref[...])

            def values(a, v_ref=v_ref):
                spread = _dot(a, expand_scr[...]).astype(BF16) * hmask_scr[...]
                return _dot(spread, v_ref[...].astype(BF16))

            value_fns.append(values)
        pv, carry = _sb_visit(zs, [mask] * len(zs), value_fns, carry_scr[...], suffix_mat)
        acc_scr[...] += pv
        carry_scr[...] = carry

    @pl.when(p == 0)
    def _():
        row = lax.broadcasted_iota(jnp.int32, (rows, SB_BLOCK), 0)
        col = lax.broadcasted_iota(jnp.int32, (rows, SB_BLOCK), 1)
        attend([(kn_ref, vn_ref)], col < row % n_new)

    @pl.when(p > 0)
    def _():
        attend([(page_refs[2 * i], page_refs[2 * i + 1]) for i in range(DECODE_PAGES)], None)

    @pl.when(p == n_steps - 1)
    def _():
        o_ref[0] = acc_scr[...]


def _sb_decode(page_table, q, bias_rows, k_new, v_new, pool_k, pool_v, layer):
    nb, n_new, hd = q.shape
    dh = hd // SB_HEADS
    n_pages = page_table.shape[1]
    rows = SB_HEADS * n_new
    prow = PAGE_SIZE * SB_HEADS
    assert n_pages % DECODE_PAGES == 0 and rows == SB_BLOCK and PAGE_SIZE == SB_BLOCK
    n_steps = n_pages // DECODE_PAGES + 1
    n_layers, n_pool = pool_v.shape[:2]
    pool_k = pool_k.reshape(n_layers, n_pool, prow, dh)
    pool_v = pool_v.reshape(n_layers, n_pool, prow, dh)
    k_new = k_new.reshape(nb, prow, dh)
    v_new = v_new.reshape(nb, prow, dh)

    def page_spec(slot):
        return pl.BlockSpec(
            (None, None, prow, dh),
            lambda b, p, pt: (layer, pt[b, n_pages - DECODE_PAGES * jnp.maximum(p, 1) + DECODE_PAGES - 1 - slot], 0, 0))

    new_spec = pl.BlockSpec((None, prow, dh), lambda b, p, pt: (b, 0, 0))
    grid_spec = pltpu.PrefetchScalarGridSpec(
        num_scalar_prefetch=1,
        grid=(nb, n_steps),
        in_specs=[
            pl.BlockSpec((1, n_new, hd), lambda b, p, pt: (b, 0, 0)),
            pl.BlockSpec((rows, SB_BLOCK), lambda b, p, pt: (0, 0)),
            new_spec, new_spec,
        ] + [page_spec(slot) for slot in range(DECODE_PAGES) for _ in range(2)],
        out_specs=pl.BlockSpec((1, rows, dh), lambda b, p, pt: (b, 0, 0)),
        scratch_shapes=[pltpu.VMEM((rows, SB_BLOCK), F32), pltpu.VMEM((rows, dh), F32),
                        pltpu.VMEM((rows, hd), BF16), pltpu.VMEM((PAGE_SIZE, prow), BF16),
                        pltpu.VMEM((rows, prow), BF16)],
    )
    return pl.pallas_call(
        functools.partial(_sb_decode_kernel, n_steps=n_steps, n_new=n_new, scale=dh ** -0.5),
        grid_spec=grid_spec,
        out_shape=jax.ShapeDtypeStruct((nb, rows, dh), F32),
        compiler_params=_params(("parallel", "arbitrary")),
        name="stickbreak_decode",
    )(page_table, q, bias_rows, k_new, v_new, *([pool_k, pool_v] * DECODE_PAGES))


def _ffn(x, mods, g_pre, g_post, w_in, w_out, layer, tf=512):
    d_ff = w_out.shape[1]
    (act,) = _modmm(x, g_pre, mods[3], mods[4], w_in, layer, (0, d_ff), tf, d_ff // tf,
                    (BF16,), _swiglu_epilogue)
    return _mm_norm_res(act, w_out, layer, x, mods[5], g_post)


def _mlstm_layer(x, mods, g_norm, w_in, w_out, layer, b_gate, g_head, c0, n0, m0, chunk_rows):
    D = x.shape[2]
    qk_tot = (w_in.shape[2] - 2 * MH) // 3
    dqk = qk_tot // (2 * MH)
    tn = PROJ_TILE
    w_gate = jnp.zeros((D, 2 * LANES), F32)
    w_gate = w_gate.at[:, :MH].set(w_in[layer, :, 3 * qk_tot:3 * qk_tot + MH])
    w_gate = w_gate.at[:, LANES:LANES + MH].set(w_in[layer, :, 3 * qk_tot + MH:])
    bias_row = jnp.zeros((1, 2 * LANES), F32)
    bias_row = bias_row.at[0, :MH].set(b_gate[0].astype(F32))
    bias_row = bias_row.at[0, LANES:LANES + MH].set(b_gate[1].astype(F32))
    epilogue = [functools.partial(_scaled_q_then_k, q_tiles=(MH * dqk) // tn, q_scale=dqk ** -0.5),
                _as_bf16, _as_f32]
    qk, v, og, gates = _modmm(x, g_norm[0], mods[0], mods[1], w_in, layer, (0, qk_tot, 2 * qk_tot), tn,
                              qk_tot // tn, (BF16, BF16, F32), epilogue, w_gate=w_gate)
    m0p = jnp.zeros((m0.shape[0], 1, LANES), F32).at[:, 0, :MH].set(m0.astype(F32))
    if chunk_rows == MLSTM_CHUNK:
        h, c_new, n_new, m_new = _mlstm(qk, v, og, gates, bias_row, g_head,
                                        c0.astype(F32), n0.astype(F32), m0p, MLSTM_CHUNK)
    else:
        nb = x.shape[1] // chunk_rows

        def pad(a):
            a = a.reshape(nb, chunk_rows, a.shape[2])
            return jnp.pad(a, ((0, 0), (0, MLSTM_CHUNK - chunk_rows), (0, 0)))

        h, c_new, n_new, m_new = _mlstm(pad(qk), pad(v), pad(og), pad(gates), bias_row, g_head,
                                        c0.astype(F32), n0.astype(F32), m0p, chunk_rows)
        h = h[:, :chunk_rows].reshape(1, nb * chunk_rows, h.shape[2])
    x = _mm_norm_res(h, w_out, layer, x, mods[2], g_norm[1])
    return x, c_new, n_new, m_new[:, 0, :MH]


def kernel(x_prompt, x_sample, state_C, state_n, state_m, cache_k, cache_v, page_table, c_prompt, c_sample, w_ada, b_ada, g_norm, w_in_a, b_gate_a, g_head_a, w_out_a, w_in_b, b_sb, w_out_b, w_ffn_in, w_ffn_out):
    bp, seq, D = x_prompt.shape
    bs, dec_seq, _ = x_sample.shape
    depth = w_ada.shape[0]
    dh = D // SB_HEADS

    rows = -(-(bp + bs) // 8) * 8
    c_all = jnp.zeros((rows, D), F32).at[:bp].set(c_prompt).at[bp:bp + bs].set(c_sample)
    ada = _ada(c_all, w_ada, b_ada)

    w_in_a, w_out_a, w_in_b, w_out_b, w_ffn_out = (
        w.astype(BF16) for w in (w_in_a, w_out_a, w_in_b, w_out_b, w_ffn_out))

    xp = x_prompt
    xs = x_sample.reshape(1, bs * dec_seq, D)
    outs = {name: [] for name in ("kp", "vp", "ks", "vs", "Cp", "np", "mp", "Cs", "ns", "ms")}
    for i in range(depth):
        j = i // 2
        mods_p = [ada[i, :bp, s * D:(s + 1) * D].reshape(bp, 1, D) for s in range(6)]
        mods_s = [jnp.repeat(ada[i, bp:bp + bs, s * D:(s + 1) * D], dec_seq, axis=0).reshape(1, bs * dec_seq, D)
                  for s in range(6)]
        if i % 2 == 0:
            dqk = (w_in_a.shape[2] - 2 * MH) // 3 // (2 * MH)
            dv = 2 * dqk
            zc = jnp.zeros((bp, MH, dqk, dv), F32)
            zn = jnp.zeros((bp, MH, dqk), F32)
            zm = jnp.zeros((bp, MH), F32)
            xp, Cp, n_p, m_p = _mlstm_layer(xp, mods_p, g_norm[i], w_in_a, w_out_a, j, b_gate_a[j], g_head_a[j],
                                            zc, zn, zm, MLSTM_CHUNK)
            xs, Cs, n_s, m_s = _mlstm_layer(xs, mods_s, g_norm[i], w_in_a, w_out_a, j, b_gate_a[j], g_head_a[j],
                                            state_C[j], state_n[j], state_m[j], dec_seq)
            outs["Cp"].append(Cp); outs["np"].append(n_p); outs["mp"].append(m_p)
            outs["Cs"].append(Cs); outs["ns"].append(n_s); outs["ms"].append(m_s)
        else:
            tn = PROJ_TILE
            bias = b_sb[j].astype(F32)
            qp, kp, vp = _modmm(xp, g_norm[i, 0], mods_p[0], mods_p[1], w_in_b, j, (0, D, 2 * D), tn,
                                D // tn, (BF16, F32, F32), [_as_bf16, _as_f32, _as_f32])
            op = _sb_prompt(qp, kp, vp, jnp.repeat(bias, dh).reshape(1, D))
            xp = _mm_norm_res(op, w_out_b, j, xp, mods_p[2], g_norm[i, 1])
            qs, kn, vn = _modmm(xs, g_norm[i, 0], mods_s[0], mods_s[1], w_in_b, j, (0, D, 2 * D), tn,
                                D // tn, (F32, F32, F32), [_as_f32, _as_f32, _as_f32])

            def as_page(a):
                a = a.reshape(bs, dec_seq, SB_HEADS, dh)
                return jnp.pad(a, ((0, 0), (0, PAGE_SIZE - dec_seq), (0, 0), (0, 0)))

            os_ = _sb_decode(page_table, qs.reshape(bs, dec_seq, D),
                             jnp.broadcast_to(jnp.repeat(bias, dec_seq)[:, None], (SB_HEADS * dec_seq, SB_BLOCK)),
                             as_page(kn), as_page(vn), cache_k, cache_v, j)
            os_ = os_.reshape(bs, SB_HEADS, dec_seq, dh).transpose(0, 2, 1, 3).reshape(1, bs * dec_seq, D)
            xs = _mm_norm_res(os_.astype(BF16), w_out_b, j, xs, mods_s[2], g_norm[i, 1])
            outs["kp"].append(kp.reshape(bp, seq, SB_HEADS, dh)); outs["vp"].append(vp.reshape(bp, seq, SB_HEADS, dh))
            outs["ks"].append(kn.reshape(bs, dec_seq, SB_HEADS, dh)); outs["vs"].append(vn.reshape(bs, dec_seq, SB_HEADS, dh))
        xp = _ffn(xp, mods_p, g_norm[i, 2], g_norm[i, 3], w_ffn_in, w_ffn_out, i)
        xs = _ffn(xs, mods_s, g_norm[i, 2], g_norm[i, 3], w_ffn_in, w_ffn_out, i)

    st = lambda name: jnp.stack(outs[name])
    return (xp, xs.reshape(bs, dec_seq, D), st("kp"), st("vp"), st("ks"), st("vs"),
            st("Cp"), st("np"), st("mp"), st("Cs"), st("ns"), st("ms"))
```

```python
import functools

import jax
import jax.numpy as jnp
from jax import lax
from jax.experimental import pallas as pl
from jax.experimental.pallas import tpu as pltpu

F32 = jnp.float32
BF16 = jnp.bfloat16

EPS = 1e-6
MH = 8
MLSTM_CHUNK = 128
SB_HEADS = 16
SB_BLOCK = 128
SB_QTILE = 512
PAGE_SIZE = 128
NEG_BIG = -1e30
LOG2E = 1.4426950408889634

LANES = 128
SUBLANES = 8
VMEM_LIMIT = 52 * 1024 * 1024
ACC_COLS = 512
PROJ_TILE = 512
DECODE_PAGES = 8
W_BLOCK_BYTES = 6 * 1024 * 1024


def _dot(a, b):
    return jnp.dot(a, b, preferred_element_type=F32)


def _dot_nt(a, b):
    return lax.dot_general(a, b, (((1,), (1,)), ((), ())), preferred_element_type=F32)


def _dot_tn(a, b):
    return lax.dot_general(a, b, (((0,), (0,)), ((), ())), preferred_element_type=F32)


def _params(sem):
    return pltpu.CompilerParams(dimension_semantics=sem, vmem_limit_bytes=VMEM_LIMIT)


def _ada_kernel(c_ref, w_ref, b_ref, o_ref):
    c = c_ref[...]
    a = (c * jax.nn.sigmoid(c)).astype(BF16)
    o_ref[0] = _dot(a, w_ref[0].astype(BF16)) + b_ref[0]


def _ada(c_all, w_ada, b_ada, tn=1024):
    depth, d, n = w_ada.shape
    rows = c_all.shape[0]
    return pl.pallas_call(
        _ada_kernel,
        grid=(depth, n // tn),
        in_specs=[
            pl.BlockSpec((rows, d), lambda l, j: (0, 0)),
            pl.BlockSpec((1, d, tn), lambda l, j: (l, 0, j)),
            pl.BlockSpec((1, 1, tn), lambda l, j: (l, 0, j)),
        ],
        out_specs=pl.BlockSpec((1, rows, tn), lambda l, j: (l, 0, j)),
        out_shape=jax.ShapeDtypeStruct((depth, rows, n), F32),
        compiler_params=_params(("parallel", "parallel")),
        name="ada_params",
    )(c_all, w_ada, b_ada.reshape(depth, 1, n))


def _row_chunks(tl):
    rc = min(tl, 256)
    return rc, tl // rc


def _for_row_chunks(tl, body):
    rc, n_chunks = _row_chunks(tl)
    if n_chunks == 1:
        body(pl.ds(0, rc))
    else:
        pl.loop(0, n_chunks)(lambda c: body(pl.ds(pl.multiple_of(c * rc, rc), rc)))


def _store_inv_rms(read_rows, r_scr, tl):
    def body(rows):
        v = read_rows(rows)
        r = lax.rsqrt(jnp.mean(v * v, axis=-1, keepdims=True) + EPS)
        r_scr[rows, :] = jnp.broadcast_to(r, (r.shape[0], LANES))

    _for_row_chunks(tl, body)


def _modmm_kernel(*refs, n_slices, n_tiles, epilogue, sequential, has_gate, tl, per_row):
    x_ref, g_ref, sh_ref, sc_ref = refs[:4]
    n_w = 1 if sequential else n_slices
    w_refs = refs[4:4 + n_w]
    pos = 4 + n_w
    if has_gate:
        wg_ref = refs[pos]
        pos += 1
    n_out = len(refs) - pos - 2 - (1 if has_gate else 0)
    out_refs = refs[pos:pos + n_out]
    pos += n_out
    if has_gate:
        go_ref = refs[pos]
        pos += 1
    h_scr, r_scr = refs[pos], refs[pos + 1]
    j = pl.program_id(2)

    @pl.when(j == 0)
    def _():
        _store_inv_rms(lambda rows: x_ref[0, rows, :], r_scr, tl)

        def modulate(rows):
            if per_row:
                sc, sh = sc_ref[0, rows, :], sh_ref[0, rows, :]
            else:
                sc, sh = sc_ref[0], sh_ref[0]
            gain = g_ref[0] * (1.0 + sc)
            h = (x_ref[0, rows, :] * r_scr[rows, :1]) * gain + sh
            h_scr[rows, :] = h.astype(BF16)

        _for_row_chunks(tl, modulate)
        if has_gate:
            go_ref[0] = _dot(h_scr[...], wg_ref[...].astype(BF16))

    if sequential:
        for s, out_ref in enumerate(out_refs):
            @pl.when(j // n_tiles == s)
            def _(s=s, out_ref=out_ref):
                y = _dot(h_scr[...], w_refs[0][...].astype(BF16))
                out_ref[0] = epilogue[s](y, j - s * n_tiles)
    else:
        hb = h_scr[...]
        ys = [_dot(hb, w_ref[...].astype(BF16)) for w_ref in w_refs]
        epilogue(ys, out_refs, j)


def _modmm(x, g, shift, scale, w, layer, slice_starts, tn, n_tiles, out_dtypes, epilogue,
           w_gate=None, tl=1024):
    nb, L, D = x.shape
    tl = min(tl, L)
    per_row = shift.shape[1] != 1
    r = tl if per_row else 1
    n_slices = len(slice_starts)
    has_gate = w_gate is not None
    sequential = isinstance(epilogue, (list, tuple))
    if sequential:
        assert all(s0 == slice_starts[0] + s * n_tiles * tn for s, s0 in enumerate(slice_starts))

    def mod_map(b, i, j):
        return (b, i if per_row else 0, 0)

    in_specs = [
        pl.BlockSpec((1, tl, D), lambda b, i, j: (b, i, 0)),
        pl.BlockSpec((1, 1, D), lambda b, i, j: (0, 0, 0)),
        pl.BlockSpec((1, r, D), mod_map),
        pl.BlockSpec((1, r, D), mod_map),
    ]
    args = [x, g.reshape(1, 1, D), shift, scale]
    for s0 in slice_starts[:1] if sequential else slice_starts:
        off = s0 // tn
        in_specs.append(pl.BlockSpec((None, D, tn), lambda b, i, j, off=off: (layer, 0, off + j)))
        args.append(w)
    if has_gate:
        gw = w_gate.shape[1]
        in_specs.append(pl.BlockSpec((D, gw), lambda b, i, j: (0, 0)))
        args.append(w_gate)
    if sequential:
        out_specs = [
            pl.BlockSpec((1, tl, tn), lambda b, i, j, s=s: (b, i, jnp.clip(j - s * n_tiles, 0, n_tiles - 1)))
            for s in range(n_slices)]
    else:
        out_specs = [pl.BlockSpec((1, tl, tn), lambda b, i, j: (b, i, j)) for _ in out_dtypes]
    out_shape = [jax.ShapeDtypeStruct((nb, L, n_tiles * tn), dt) for dt in out_dtypes]
    if has_gate:
        out_specs.append(pl.BlockSpec((1, tl, gw), lambda b, i, j: (b, i, 0)))
        out_shape.append(jax.ShapeDtypeStruct((nb, L, gw), F32))
    return pl.pallas_call(
        functools.partial(_modmm_kernel, n_slices=n_slices, n_tiles=n_tiles, epilogue=epilogue,
                          sequential=sequential, has_gate=has_gate, tl=tl, per_row=per_row),
        grid=(nb, L // tl, n_tiles * n_slices if sequential else n_tiles),
        in_specs=in_specs,
        out_specs=out_specs,
        out_shape=out_shape,
        scratch_shapes=[pltpu.VMEM((tl, D), BF16), pltpu.VMEM((tl, LANES), F32)],
        compiler_params=_params(("parallel", "parallel", "arbitrary")),
        name="modulated_matmul",
    )(*args)


def _as_bf16(y, tile):
    return y.astype(BF16)


def _as_f32(y, tile):
    return y


def _scaled_q_then_k(y, tile, *, q_cols, q_scale):
    tn = y.shape[1]
    if q_cols % tn == 0:
        factor = jnp.where(tile < q_cols // tn, q_scale, 1.0).astype(F32)
    else:
        col = tile * tn + lax.broadcasted_iota(jnp.int32, (1, tn), 1)
        factor = jnp.where(col < q_cols, q_scale, 1.0).astype(F32)
    return (y * factor).astype(BF16)


def _proj_tile(rows, width):
    return PROJ_TILE if rows >= 2 * PROJ_TILE else width


def _swiglu_epilogue(ys, out_refs, j):
    gate, up = ys
    out_refs[0][0] = ((gate * jax.nn.sigmoid(gate)) * up).astype(BF16)


def _mm_norm_res_kernel(a_ref, w_ref, x_ref, gate_ref, g_ref, o_ref, r_scr, *, tl, nk, per_row):
    k = pl.program_id(2)

    @pl.when(k == 0)
    def _():
        o_ref[...] = jnp.zeros_like(o_ref)

    a = a_ref[0]
    cw = min(ACC_COLS, o_ref.shape[2])
    for c in range(o_ref.shape[2] // cw):
        cols = slice(c * cw, (c + 1) * cw)
        o_ref[0, :, cols] += _dot(a, w_ref[:, cols].astype(BF16))

    @pl.when(k == nk - 1)
    def _():
        _store_inv_rms(lambda rows: o_ref[0, rows, :], r_scr, tl)

        def finish(rows):
            gate = gate_ref[0, rows, :] if per_row else gate_ref[0]
            gain = gate * g_ref[0]
            o_ref[0, rows, :] = x_ref[0, rows, :] + (o_ref[0, rows, :] * r_scr[rows, :1]) * gain

        _for_row_chunks(tl, finish)


def _k_tile(K, D, itemsize):
    limit = max(W_BLOCK_BYTES // (D * itemsize), LANES)
    return max(t for t in range(LANES, K + 1, LANES) if K % t == 0 and t <= limit)


def _mm_norm_res(a, w, layer, x, gate, g, tl=1024):
    nb, L, K = a.shape
    D = w.shape[2]
    tl = min(tl, L)
    per_row = gate.shape[1] != 1
    r = tl if per_row else 1
    tk = _k_tile(K, D, w.dtype.itemsize)
    nk = K // tk
    return pl.pallas_call(
        functools.partial(_mm_norm_res_kernel, tl=tl, nk=nk, per_row=per_row),
        grid=(nb, L // tl, nk),
        in_specs=[
            pl.BlockSpec((1, tl, tk), lambda b, i, k: (b, i, k)),
            pl.BlockSpec((None, tk, D), lambda b, i, k: (layer, k, 0)),
            pl.BlockSpec((1, tl, D), lambda b, i, k: (b, i, 0)),
            pl.BlockSpec((1, r, D), lambda b, i, k: (b, i if per_row else 0, 0)),
            pl.BlockSpec((1, 1, D), lambda b, i, k: (0, 0, 0)),
        ],
        out_specs=pl.BlockSpec((1, tl, D), lambda b, i, k: (b, i, 0)),
        out_shape=jax.ShapeDtypeStruct((nb, L, D), F32),
        scratch_shapes=[pltpu.VMEM((tl, LANES), F32)],
        compiler_params=_params(("parallel", "parallel", "arbitrary")),
        name="matmul_norm_residual",
    )(a, w, x, gate, g.reshape(1, 1, D))


def _scan_rows(x, op, row):
    d = 1
    while d < x.shape[0]:
        shifted = pltpu.roll(x, d, axis=0)
        x = jnp.where(row >= d, op(x, shifted), x)
        d *= 2
    return x


def _mlstm_kernel(qk_ref, v_ref, og_ref, gt_ref, bias_ref, gh_ref, c0_ref, n0_ref, m0_ref,
                  o_ref, c_out_ref, n_out_ref, m_out_ref, c_scr, n_scr, m_scr,
                  *, valid_len, nc, dqk, dv):
    c = pl.program_id(1)
    L = MLSTM_CHUNK

    @pl.when(c == 0)
    def _():
        c_scr[...] = c0_ref[0]
        n_scr[...] = n0_ref[0]
        m_scr[...] = m0_ref[0]

    gates = gt_ref[0] + bias_ref[...]
    li = gates[:, :LANES]
    gf = gates[:, LANES:]
    lf = jnp.minimum(gf, 0.0) - jnp.log1p(jnp.exp(-jnp.abs(gf)))
    row = lax.broadcasted_iota(jnp.int32, (L, LANES), 0)
    col = lax.broadcasted_iota(jnp.int32, (L, LANES), 1)
    if valid_len < L:
        li = jnp.where(row < valid_len, li, NEG_BIG)
        lf = jnp.where(row < valid_len, lf, 0.0)

    b = _scan_rows(lf, jnp.add, row)
    u = li - b
    m_prev = m_scr[...]
    m_t = b + jnp.maximum(m_prev, _scan_rows(u, jnp.maximum, row))
    d_inter = jnp.exp(b + m_prev - m_t)
    ct = b - m_t
    inv_floor = jnp.exp(-m_t)
    b_end = b[L - 1:L, :]
    m_end = m_t[L - 1:L, :]
    w_all = jnp.exp(b_end - b + li - m_end)
    decay = jnp.exp(b_end + m_prev - m_end)
    u_t = u.T
    causal = row >= col

    for h in range(MH):
        q = qk_ref[0, :, h * dqk:(h + 1) * dqk]
        k = qk_ref[0, :, MH * dqk + h * dqk:MH * dqk + (h + 1) * dqk]
        v = v_ref[0, :, h * dv:(h + 1) * dv]
        log_d = jnp.where(causal, ct[:, h:h + 1] + u_t[h:h + 1, :], NEG_BIG)
        s = _dot_nt(q, k) * jnp.exp(log_d)
        di = d_inter[:, h:h + 1]
        c_prev = c_scr[h]
        num = _dot(s.astype(BF16), v) + di * _dot(q, c_prev.astype(BF16))
        n_prev = n_scr[h:h + 1, :]
        qn_state = jnp.sum(q.astype(F32) * n_prev.astype(BF16).astype(F32), axis=-1, keepdims=True)
        qn = jnp.sum(s, axis=-1, keepdims=True) + di * qn_state
        hh = num / jnp.maximum(jnp.abs(qn), inv_floor[:, h:h + 1])
        hh = hh * lax.rsqrt(jnp.mean(hh * hh, axis=-1, keepdims=True) + EPS)
        hh = hh * gh_ref[:, h * dv:(h + 1) * dv]
        og = og_ref[0, :, h * dv:(h + 1) * dv]
        o_ref[0, :, h * dv:(h + 1) * dv] = (hh * jax.nn.sigmoid(og)).astype(BF16)

        kw = k.astype(F32) * w_all[:, h:h + 1]
        dec = decay[:, h:h + 1]
        c_scr[h] = dec * c_prev + _dot_tn(kw.astype(BF16), v)
        n_scr[h:h + 1, :] = dec * n_prev + jnp.sum(kw, axis=0, keepdims=True)

    m_scr[...] = m_end

    @pl.when(c == nc - 1)
    def _():
        c_out_ref[0] = c_scr[...]
        n_out_ref[0] = n_scr[...]
        m_out_ref[0] = m_scr[...]


def _mlstm(qk, v, og, gates, bias_row, g_head, c0, n0, m0, valid_len):
    nb, L, vt = v.shape
    dv = vt // MH
    dqk = qk.shape[2] // (2 * MH)
    nc = L // MLSTM_CHUNK
    blk = lambda width: pl.BlockSpec((1, MLSTM_CHUNK, width), lambda b, c: (b, c, 0))
    return pl.pallas_call(
        functools.partial(_mlstm_kernel, valid_len=valid_len, nc=nc, dqk=dqk, dv=dv),
        grid=(nb, nc),
        in_specs=[
            blk(qk.shape[2]), blk(vt), blk(vt), blk(2 * LANES),
            pl.BlockSpec((1, 2 * LANES), lambda b, c: (0, 0)),
            pl.BlockSpec((1, vt), lambda b, c: (0, 0)),
            pl.BlockSpec((1, MH, dqk, dv), lambda b, c: (b, 0, 0, 0)),
            pl.BlockSpec((1, MH, dqk), lambda b, c: (b, 0, 0)),
            pl.BlockSpec((1, 1, LANES), lambda b, c: (b, 0, 0)),
        ],
        out_specs=[
            blk(vt),
            pl.BlockSpec((1, MH, dqk, dv), lambda b, c: (b, 0, 0, 0)),
            pl.BlockSpec((1, MH, dqk), lambda b, c: (b, 0, 0)),
            pl.BlockSpec((1, 1, LANES), lambda b, c: (b, 0, 0)),
        ],
        out_shape=[
            jax.ShapeDtypeStruct((nb, L, vt), BF16),
            jax.ShapeDtypeStruct((nb, MH, dqk, dv), F32),
            jax.ShapeDtypeStruct((nb, MH, dqk), F32),
            jax.ShapeDtypeStruct((nb, 1, LANES), F32),
        ],
        scratch_shapes=[
            pltpu.VMEM((MH, dqk, dv), F32),
            pltpu.VMEM((MH, dqk), F32),
            pltpu.VMEM((1, LANES), F32),
        ],
        compiler_params=_params(("parallel", "arbitrary")),
        name="mlstm_chunks",
    )(qk, v, og, gates, bias_row, g_head.reshape(1, vt), c0, n0, m0)


def _suffix_matrix():
    j = lax.broadcasted_iota(jnp.int32, (2 * SB_BLOCK, 2 * SB_BLOCK), 0) % SB_BLOCK
    s = lax.broadcasted_iota(jnp.int32, (2 * SB_BLOCK, 2 * SB_BLOCK), 1)
    return jnp.where((s >= SB_BLOCK) | (j > s), 1.0, 0.0).astype(BF16)


def _sb_scores(z, mask, suffix_mat):
    soft = jnp.log2(1.0 + jnp.exp2(-jnp.abs(z)))
    log_beta = jnp.minimum(z, 0.0) - soft
    log_1mb = log_beta - z
    if mask is not None:
        log_1mb = jnp.where(mask, log_1mb, 0.0)
    hi = log_1mb.astype(BF16)
    lo = (log_1mb - hi.astype(F32)).astype(BF16)
    sums = _dot(jnp.concatenate([hi, lo], axis=1), suffix_mat)
    return log_beta, sums[:, :SB_BLOCK], sums[:, SB_BLOCK:]


def _sb_visit(zs, masks, value_fns, carry, suffix_mat):
    scores = [_sb_scores(z, m, suffix_mat) for z, m in zip(zs, masks)]
    pv = None
    for (log_beta, later, total), mask, value_fn in zip(scores, masks, value_fns):
        a = jnp.exp2(log_beta + later + carry)
        if mask is not None:
            a = jnp.where(mask, a, 0.0)
        out = value_fn(a.astype(BF16))
        pv = out if pv is None else pv + out
        carry = carry + total
    return pv, carry


def _sb_prompt_kernel(q_ref, k_ref, v_ref, bias_ref, o_ref, k_scr, v_scr, acc_scr, carry_scr,
                      *, n_qt, scale):
    k_scr[...] = k_ref[0].astype(BF16)
    v_scr[...] = v_ref[0].astype(BF16)
    suffix_mat = _suffix_matrix()
    bias = bias_ref[...] * LOG2E
    sub = SB_QTILE // SB_BLOCK

    def visit(r0, r1, blocks, mask):
        q = q_ref[0, r0:r1, :]
        zs, value_fns = [], []
        for j in blocks:
            keys = pl.ds(pl.multiple_of(j * SB_BLOCK, SB_BLOCK), SB_BLOCK)
            zs.append(_dot_nt(q, k_scr[keys, :]) * (scale * LOG2E) + bias)
            value_fns.append(lambda a, keys=keys: _dot(a, v_scr[keys, :]))
        pv, carry = _sb_visit(zs, [mask] * len(zs), value_fns, carry_scr[r0:r1, :], suffix_mat)
        acc_scr[r0:r1, :] += pv
        carry_scr[r0:r1, :] = carry

    acc_scr[...] = jnp.zeros_like(acc_scr)
    carry_scr[...] = jnp.zeros_like(carry_scr)
    for c in reversed(range(sub)):
        rows_left = SB_QTILE - c * SB_BLOCK
        row = lax.broadcasted_iota(jnp.int32, (rows_left, SB_BLOCK), 0)
        col = lax.broadcasted_iota(jnp.int32, (rows_left, SB_BLOCK), 1)
        for t in range(n_qt):
            visit(t * SB_QTILE + c * SB_BLOCK, (t + 1) * SB_QTILE, [t * sub + c], col < row)

    for t in range(1, n_qt):
        def older(i, _, t=t):
            newest = (t - i) * sub - 1
            visit(t * SB_QTILE, (t + 1) * SB_QTILE, [newest - d for d in range(sub)], None)
            return 0

        lax.fori_loop(0, t, older, 0)
    o_ref[0] = acc_scr[...].astype(BF16)


def _sb_prompt(q, k, v, bias_lanes):
    nb, L, hd = q.shape
    dh = hd // SB_HEADS
    spec = pl.BlockSpec((1, L, dh), lambda b, h: (b, 0, h))
    return pl.pallas_call(
        functools.partial(_sb_prompt_kernel, n_qt=L // SB_QTILE, scale=dh ** -0.5),
        grid=(nb, SB_HEADS),
        in_specs=[spec, spec, spec, pl.BlockSpec((1, dh), lambda b, h: (0, h))],
        out_specs=spec,
        out_shape=jax.ShapeDtypeStruct((nb, L, hd), BF16),
        scratch_shapes=[pltpu.VMEM((L, dh), BF16), pltpu.VMEM((L, dh), BF16),
                        pltpu.VMEM((L, dh), F32), pltpu.VMEM((L, SB_BLOCK), F32)],
        compiler_params=_params(("parallel", "parallel")),
        name="stickbreak_prompt",
    )(q, k, v, bias_lanes)


def _sb_decode_kernel(pt_ref, q_ref, bias_ref, kn_ref, vn_ref, *rest, n_steps, n_new, scale):
    page_refs = rest[:2 * DECODE_PAGES]
    o_ref, carry_scr, acc_scr, qbd_scr, expand_scr, hmask_scr = rest[2 * DECODE_PAGES:]
    p = pl.program_id(1)
    rows = SB_HEADS * n_new
    dh = q_ref.shape[2] // SB_HEADS
    suffix_mat = _suffix_matrix()

    @pl.when(p == 0)
    def _():
        carry_scr[...] = jnp.zeros_like(carry_scr)
        acc_scr[...] = jnp.zeros_like(acc_scr)
        q_rep = jnp.concatenate([q_ref[0]] * SB_HEADS, axis=0)
        lane_head = lax.broadcasted_iota(jnp.int32, q_rep.shape, 1) // dh
        row_head = lax.broadcasted_iota(jnp.int32, q_rep.shape, 0) // n_new
        qbd_scr[...] = jnp.where(lane_head == row_head, q_rep, 0.0).astype(BF16)
        tok = lax.broadcasted_iota(jnp.int32, expand_scr.shape, 0)
        c = lax.broadcasted_iota(jnp.int32, expand_scr.shape, 1)
        expand_scr[...] = jnp.where(c // SB_HEADS == tok, 1.0, 0.0).astype(BF16)
        r = lax.broadcasted_iota(jnp.int32, hmask_scr.shape, 0)
        c = lax.broadcasted_iota(jnp.int32, hmask_scr.shape, 1)
        hmask_scr[...] = jnp.where(c % SB_HEADS == r // n_new, 1.0, 0.0).astype(BF16)

    def attend(kv_refs, mask):
        zs, value_fns = [], []
        for k_ref, v_ref in kv_refs:
            k_all = jnp.concatenate(
                [k_ref[pl.ds(h, PAGE_SIZE, stride=SB_HEADS), :].astype(BF16) for h in range(SB_HEADS)],
                axis=1)
            zs.append(_dot_nt(qbd_scr[...], k_all) * (scale * LOG2E) + bias_ref[...] * LOG2E)

            def values(a, v_ref=v_ref):
                spread = _dot(a, expand_scr[...]).astype(BF16) * hmask_scr[...]
                return _dot(spread, v_ref[...].astype(BF16))

            value_fns.append(values)
        pv, carry = _sb_visit(zs, [mask] * len(zs), value_fns, carry_scr[...], suffix_mat)
        acc_scr[...] += pv
        carry_scr[...] = carry

    @pl.when(p == 0)
    def _():
        row = lax.broadcasted_iota(jnp.int32, (rows, SB_BLOCK), 0)
        col = lax.broadcasted_iota(jnp.int32, (rows, SB_BLOCK), 1)
        attend([(kn_ref, vn_ref)], col < row % n_new)

    @pl.when(p > 0)
    def _():
        attend([(page_refs[2 * i], page_refs[2 * i + 1]) for i in range(DECODE_PAGES)], None)

    @pl.when(p == n_steps - 1)
    def _():
        o_ref[0] = acc_scr[...]


def _sb_decode(page_table, q, bias_rows, k_new, v_new, pool_k, pool_v, layer):
    nb, n_new, hd = q.shape
    dh = hd // SB_HEADS
    n_pages = page_table.shape[1]
    rows = SB_HEADS * n_new
    prow = PAGE_SIZE * SB_HEADS
    assert n_pages % DECODE_PAGES == 0 and rows == SB_BLOCK and PAGE_SIZE == SB_BLOCK
    n_steps = n_pages // DECODE_PAGES + 1
    n_layers, n_pool = pool_v.shape[:2]
    pool_k = pool_k.reshape(n_layers, n_pool, prow, dh)
    pool_v = pool_v.reshape(n_layers, n_pool, prow, dh)
    k_new = k_new.reshape(nb, prow, dh)
    v_new = v_new.reshape(nb, prow, dh)

    def page_spec(slot):
        return pl.BlockSpec(
            (None, None, prow, dh),
            lambda b, p, pt: (layer, pt[b, n_pages - DECODE_PAGES * jnp.maximum(p, 1) + DECODE_PAGES - 1 - slot], 0, 0))

    new_spec = pl.BlockSpec((None, prow, dh), lambda b, p, pt: (b, 0, 0))
    grid_spec = pltpu.PrefetchScalarGridSpec(
        num_scalar_prefetch=1,
        grid=(nb, n_steps),
        in_specs=[
            pl.BlockSpec((1, n_new, hd), lambda b, p, pt: (b, 0, 0)),
            pl.BlockSpec((rows, SB_BLOCK), lambda b, p, pt: (0, 0)),
            new_spec, new_spec,
        ] + [page_spec(slot) for slot in range(DECODE_PAGES) for _ in range(2)],
        out_specs=pl.BlockSpec((1, rows, dh), lambda b, p, pt: (b, 0, 0)),
        scratch_shapes=[pltpu.VMEM((rows, SB_BLOCK), F32), pltpu.VMEM((rows, dh), F32),
                        pltpu.VMEM((rows, hd), BF16), pltpu.VMEM((PAGE_SIZE, prow), BF16),
                        pltpu.VMEM((rows, prow), BF16)],
    )
    return pl.pallas_call(
        functools.partial(_sb_decode_kernel, n_steps=n_steps, n_new=n_new, scale=dh ** -0.5),
        grid_spec=grid_spec,
        out_shape=jax.ShapeDtypeStruct((nb, rows, dh), F32),
        compiler_params=_params(("parallel", "arbitrary")),
        name="stickbreak_decode",
    )(page_table, q, bias_rows, k_new, v_new, *([pool_k, pool_v] * DECODE_PAGES))


def _ffn(x, mods, g_pre, g_post, w_in, w_out, layer, tf=512):
    d_ff = w_out.shape[1]
    (act,) = _modmm(x, g_pre, mods[3], mods[4], w_in, layer, (0, d_ff), tf, d_ff // tf,
                    (BF16,), _swiglu_epilogue)
    return _mm_norm_res(act, w_out, layer, x, mods[5], g_post)


def _mlstm_layer(x, mods, g_norm, w_in, w_out, layer, b_gate, g_head, c0, n0, m0, chunk_rows):
    D = x.shape[2]
    qk_tot = (w_in.shape[2] - 2 * MH) // 3
    dqk = qk_tot // (2 * MH)
    tn = _proj_tile(x.shape[1], qk_tot)
    w_gate = jnp.zeros((D, 2 * LANES), F32)
    w_gate = w_gate.at[:, :MH].set(w_in[layer, :, 3 * qk_tot:3 * qk_tot + MH])
    w_gate = w_gate.at[:, LANES:LANES + MH].set(w_in[layer, :, 3 * qk_tot + MH:])
    bias_row = jnp.zeros((1, 2 * LANES), F32)
    bias_row = bias_row.at[0, :MH].set(b_gate[0].astype(F32))
    bias_row = bias_row.at[0, LANES:LANES + MH].set(b_gate[1].astype(F32))
    epilogue = [functools.partial(_scaled_q_then_k, q_cols=MH * dqk, q_scale=dqk ** -0.5),
                _as_bf16, _as_f32]
    qk, v, og, gates = _modmm(x, g_norm[0], mods[0], mods[1], w_in, layer, (0, qk_tot, 2 * qk_tot), tn,
                              qk_tot // tn, (BF16, BF16, F32), epilogue, w_gate=w_gate)
    m0p = jnp.zeros((m0.shape[0], 1, LANES), F32).at[:, 0, :MH].set(m0.astype(F32))
    if chunk_rows == MLSTM_CHUNK:
        h, c_new, n_new, m_new = _mlstm(qk, v, og, gates, bias_row, g_head,
                                        c0.astype(F32), n0.astype(F32), m0p, MLSTM_CHUNK)
    else:
        nb = x.shape[1] // chunk_rows

        def pad(a):
            a = a.reshape(nb, chunk_rows, a.shape[2])
            return jnp.pad(a, ((0, 0), (0, MLSTM_CHUNK - chunk_rows), (0, 0)))

        h, c_new, n_new, m_new = _mlstm(pad(qk), pad(v), pad(og), pad(gates), bias_row, g_head,
                                        c0.astype(F32), n0.astype(F32), m0p, chunk_rows)
        h = h[:, :chunk_rows].reshape(1, nb * chunk_rows, h.shape[2])
    x = _mm_norm_res(h, w_out, layer, x, mods[2], g_norm[1])
    return x, c_new, n_new, m_new[:, 0, :MH]


def kernel(x_prompt, x_sample, state_C, state_n, state_m, cache_k, cache_v, page_table, c_prompt, c_sample, w_ada, b_ada, g_norm, w_in_a, b_gate_a, g_head_a, w_out_a, w_in_b, b_sb, w_out_b, w_ffn_in, w_ffn_out):
    bp, seq, D = x_prompt.shape
    bs, dec_seq, _ = x_sample.shape
    depth = w_ada.shape[0]
    dh = D // SB_HEADS

    rows = -(-(bp + bs) // 8) * 8
    c_all = jnp.zeros((rows, D), F32).at[:bp].set(c_prompt).at[bp:bp + bs].set(c_sample)
    ada = _ada(c_all, w_ada, b_ada)

    w_in_a, w_out_a, w_in_b, w_out_b, w_ffn_out = (
        w.astype(BF16) for w in (w_in_a, w_out_a, w_in_b, w_out_b, w_ffn_out))

    xp = x_prompt
    xs = x_sample.reshape(1, bs * dec_seq, D)
    outs = {name: [] for name in ("kp", "vp", "ks", "vs", "Cp", "np", "mp", "Cs", "ns", "ms")}
    for i in range(depth):
        j = i // 2
        mods_p = [ada[i, :bp, s * D:(s + 1) * D].reshape(bp, 1, D) for s in range(6)]
        mods_s = [jnp.repeat(ada[i, bp:bp + bs, s * D:(s + 1) * D], dec_seq, axis=0).reshape(1, bs * dec_seq, D)
                  for s in range(6)]
        if i % 2 == 0:
            dqk = (w_in_a.shape[2] - 2 * MH) // 3 // (2 * MH)
            dv = 2 * dqk
            zc = jnp.zeros((bp, MH, dqk, dv), F32)
            zn = jnp.zeros((bp, MH, dqk), F32)
            zm = jnp.zeros((bp, MH), F32)
            xp, Cp, n_p, m_p = _mlstm_layer(xp, mods_p, g_norm[i], w_in_a, w_out_a, j, b_gate_a[j], g_head_a[j],
                                            zc, zn, zm, MLSTM_CHUNK)
            xs, Cs, n_s, m_s = _mlstm_layer(xs, mods_s, g_norm[i], w_in_a, w_out_a, j, b_gate_a[j], g_head_a[j],
                                            state_C[j], state_n[j], state_m[j], dec_seq)
            outs["Cp"].append(Cp); outs["np"].append(n_p); outs["mp"].append(m_p)
            outs["Cs"].append(Cs); outs["ns"].append(n_s); outs["ms"].append(m_s)
        else:
            tn = _proj_tile(seq, D)
            bias = b_sb[j].astype(F32)
            qp, kp, vp = _modmm(xp, g_norm[i, 0], mods_p[0], mods_p[1], w_in_b, j, (0, D, 2 * D), tn,
                                D // tn, (BF16, F32, F32), [_as_bf16, _as_f32, _as_f32])
            op = _sb_prompt(qp, kp, vp, jnp.repeat(bias, dh).reshape(1, D))
            xp = _mm_norm_res(op, w_out_b, j, xp, mods_p[2], g_norm[i, 1])
            tn = _proj_tile(xs.shape[1], D)
            qs, kn, vn = _modmm(xs, g_norm[i, 0], mods_s[0], mods_s[1], w_in_b, j, (0, D, 2 * D), tn,
                                D // tn, (F32, F32, F32), [_as_f32, _as_f32, _as_f32])

            def as_page(a):
                a = a.reshape(bs, dec_seq, SB_HEADS, dh)
                return jnp.pad(a, ((0, 0), (0, PAGE_SIZE - dec_seq), (0, 0), (0, 0)))

            os_ = _sb_decode(page_table, qs.reshape(bs, dec_seq, D),
                             jnp.broadcast_to(jnp.repeat(bias, dec_seq)[:, None], (SB_HEADS * dec_seq, SB_BLOCK)),
                             as_page(kn), as_page(vn), cache_k, cache_v, j)
            os_ = os_.reshape(bs, SB_HEADS, dec_seq, dh).transpose(0, 2, 1, 3).reshape(1, bs * dec_seq, D)
            xs = _mm_norm_res(os_.astype(BF16), w_out_b, j, xs, mods_s[2], g_norm[i, 1])
            outs["kp"].append(kp.reshape(bp, seq, SB_HEADS, dh)); outs["vp"].append(vp.reshape(bp, seq, SB_HEADS, dh))
            outs["ks"].append(kn.reshape(bs, dec_seq, SB_HEADS, dh)); outs["vs"].append(vn.reshape(bs, dec_seq, SB_HEADS, dh))
        xp = _ffn(xp, mods_p, g_norm[i, 2], g_norm[i, 3], w_ffn_in, w_ffn_out, i)
        xs = _ffn(xs, mods_s, g_norm[i, 2], g_norm[i, 3], w_ffn_in, w_ffn_out, i)

    st = lambda name: jnp.stack(outs[name])
    return (xp, xs.reshape(bs, dec_seq, D), st("kp"), st("vp"), st("ks"), st("vs"),
            st("Cp"), st("np"), st("mp"), st("Cs"), st("ns"), st("ms"))
```

```python
import functools

import jax
import jax.numpy as jnp
from jax import lax
from jax.experimental import pallas as pl
from jax.experimental.pallas import tpu as pltpu

F32 = jnp.float32
BF16 = jnp.bfloat16

EPS = 1e-6
MH = 8
MLSTM_CHUNK = 128
SB_HEADS = 16
SB_BLOCK = 128
SB_QTILE = 512
PAGE_SIZE = 128
NEG_BIG = -1e30
LOG2E = 1.4426950408889634

LANES = 128
SUBLANES = 8
VMEM_LIMIT = 52 * 1024 * 1024
ACC_COLS = 512
PROJ_TILE = 512
DECODE_PAGES = 8
W_BLOCK_BYTES = 6 * 1024 * 1024


def _dot(a, b):
    return jnp.dot(a, b, preferred_element_type=F32)


def _dot_nt(a, b):
    return lax.dot_general(a, b, (((1,), (1,)), ((), ())), preferred_element_type=F32)


def _dot_tn(a, b):
    return lax.dot_general(a, b, (((0,), (0,)), ((), ())), preferred_element_type=F32)


def _params(sem):
    return pltpu.CompilerParams(dimension_semantics=sem, vmem_limit_bytes=VMEM_LIMIT)


def _ada_kernel(c_ref, w_ref, b_ref, o_ref):
    c = c_ref[...]
    a = (c * jax.nn.sigmoid(c)).astype(BF16)
    o_ref[0] = _dot(a, w_ref[0].astype(BF16)) + b_ref[0]


def _ada(c_all, w_ada, b_ada, tn=1024):
    depth, d, n = w_ada.shape
    rows = c_all.shape[0]
    return pl.pallas_call(
        _ada_kernel,
        grid=(depth, n // tn),
        in_specs=[
            pl.BlockSpec((rows, d), lambda l, j: (0, 0)),
            pl.BlockSpec((1, d, tn), lambda l, j: (l, 0, j)),
            pl.BlockSpec((1, 1, tn), lambda l, j: (l, 0, j)),
        ],
        out_specs=pl.BlockSpec((1, rows, tn), lambda l, j: (l, 0, j)),
        out_shape=jax.ShapeDtypeStruct((depth, rows, n), F32),
        compiler_params=_params(("parallel", "parallel")),
        name="ada_params",
    )(c_all, w_ada, b_ada.reshape(depth, 1, n))


def _row_chunks(tl):
    rc = min(tl, 256)
    return rc, tl // rc


def _for_row_chunks(tl, body):
    rc, n_chunks = _row_chunks(tl)
    if n_chunks == 1:
        body(pl.ds(0, rc))
    else:
        pl.loop(0, n_chunks)(lambda c: body(pl.ds(pl.multiple_of(c * rc, rc), rc)))


def _store_inv_rms(read_rows, r_scr, tl):
    def body(rows):
        v = read_rows(rows)
        r = lax.rsqrt(jnp.mean(v * v, axis=-1, keepdims=True) + EPS)
        r_scr[rows, :] = jnp.broadcast_to(r, (r.shape[0], LANES))

    _for_row_chunks(tl, body)


def _modmm_kernel(*refs, n_slices, n_tiles, epilogue, sequential, has_gate, emit_copy, tl, per_row):
    x_ref, g_ref, sh_ref, sc_ref = refs[:4]
    n_w = 1 if sequential else n_slices
    w_refs = refs[4:4 + n_w]
    pos = 4 + n_w
    if has_gate:
        wg_ref = refs[pos]
        pos += 1
    n_out = len(refs) - pos - 2 - (1 if has_gate else 0) - (1 if emit_copy else 0)
    out_refs = refs[pos:pos + n_out]
    pos += n_out
    if has_gate:
        go_ref = refs[pos]
        pos += 1
    if emit_copy:
        wcopy_ref = refs[pos]
        pos += 1
    h_scr, r_scr = refs[pos], refs[pos + 1]
    j = pl.program_id(2)

    @pl.when(j == 0)
    def _():
        _store_inv_rms(lambda rows: x_ref[0, rows, :], r_scr, tl)

        def modulate(rows):
            if per_row:
                sc, sh = sc_ref[0, rows, :], sh_ref[0, rows, :]
            else:
                sc, sh = sc_ref[0], sh_ref[0]
            gain = g_ref[0] * (1.0 + sc)
            h = (x_ref[0, rows, :] * r_scr[rows, :1]) * gain + sh
            h_scr[rows, :] = h.astype(BF16)

        _for_row_chunks(tl, modulate)
        if has_gate:
            go_ref[0] = _dot(h_scr[...], wg_ref[...].astype(BF16))

    if sequential:
        if emit_copy:
            wcopy_ref[...] = w_refs[0][...].astype(BF16)
            rhs_ref = wcopy_ref
        else:
            rhs_ref = w_refs[0]
        for s, out_ref in enumerate(out_refs):
            @pl.when(j // n_tiles == s)
            def _(s=s, out_ref=out_ref):
                y = _dot(h_scr[...], rhs_ref[...].astype(BF16))
                out_ref[0] = epilogue[s](y, j - s * n_tiles)
    else:
        hb = h_scr[...]
        ys = [_dot(hb, w_ref[...].astype(BF16)) for w_ref in w_refs]
        epilogue(ys, out_refs, j)


def _modmm(x, g, shift, scale, w, layer, slice_starts, tn, n_tiles, out_dtypes, epilogue,
           w_gate=None, tl=1024, emit_copy=False):
    nb, L, D = x.shape
    tl = min(tl, L)
    per_row = shift.shape[1] != 1
    r = tl if per_row else 1
    n_slices = len(slice_starts)
    has_gate = w_gate is not None
    sequential = isinstance(epilogue, (list, tuple))
    if sequential:
        assert all(s0 == slice_starts[0] + s * n_tiles * tn for s, s0 in enumerate(slice_starts))

    def mod_map(b, i, j):
        return (b, i if per_row else 0, 0)

    in_specs = [
        pl.BlockSpec((1, tl, D), lambda b, i, j: (b, i, 0)),
        pl.BlockSpec((1, 1, D), lambda b, i, j: (0, 0, 0)),
        pl.BlockSpec((1, r, D), mod_map),
        pl.BlockSpec((1, r, D), mod_map),
    ]
    args = [x, g.reshape(1, 1, D), shift, scale]
    for s0 in slice_starts[:1] if sequential else slice_starts:
        off = s0 // tn
        in_specs.append(pl.BlockSpec((None, D, tn), lambda b, i, j, off=off: (layer, 0, off + j)))
        args.append(w)
    if has_gate:
        gw = w_gate.shape[1]
        in_specs.append(pl.BlockSpec((D, gw), lambda b, i, j: (0, 0)))
        args.append(w_gate)
    if sequential:
        out_specs = [
            pl.BlockSpec((1, tl, tn), lambda b, i, j, s=s: (b, i, jnp.clip(j - s * n_tiles, 0, n_tiles - 1)))
            for s in range(n_slices)]
    else:
        out_specs = [pl.BlockSpec((1, tl, tn), lambda b, i, j: (b, i, j)) for _ in out_dtypes]
    out_shape = [jax.ShapeDtypeStruct((nb, L, n_tiles * tn), dt) for dt in out_dtypes]
    if has_gate:
        out_specs.append(pl.BlockSpec((1, tl, gw), lambda b, i, j: (b, i, 0)))
        out_shape.append(jax.ShapeDtypeStruct((nb, L, gw), F32))
    if emit_copy:
        assert sequential and nb == 1 and L == tl and slice_starts[0] == 0
        out_specs.append(pl.BlockSpec((None, D, tn), lambda b, i, j: (0, 0, j)))
        out_shape.append(jax.ShapeDtypeStruct((1, D, n_slices * n_tiles * tn), BF16))
    return pl.pallas_call(
        functools.partial(_modmm_kernel, n_slices=n_slices, n_tiles=n_tiles, epilogue=epilogue,
                          sequential=sequential, has_gate=has_gate, emit_copy=emit_copy, tl=tl,
                          per_row=per_row),
        grid=(nb, L // tl, n_tiles * n_slices if sequential else n_tiles),
        in_specs=in_specs,
        out_specs=out_specs,
        out_shape=out_shape,
        scratch_shapes=[pltpu.VMEM((tl, D), BF16), pltpu.VMEM((tl, LANES), F32)],
        compiler_params=_params(("parallel", "parallel", "arbitrary")),
        name="modulated_matmul",
    )(*args)


def _as_bf16(y, tile):
    return y.astype(BF16)


def _as_f32(y, tile):
    return y


def _scaled_q_then_k(y, tile, *, q_cols, q_scale):
    tn = y.shape[1]
    if q_cols % tn == 0:
        factor = jnp.where(tile < q_cols // tn, q_scale, 1.0).astype(F32)
    else:
        col = tile * tn + lax.broadcasted_iota(jnp.int32, (1, tn), 1)
        factor = jnp.where(col < q_cols, q_scale, 1.0).astype(F32)
    return (y * factor).astype(BF16)


def _proj_tile(rows, width, itemsize):
    if rows >= 2 * PROJ_TILE:
        return PROJ_TILE
    return width if itemsize <= 2 else width // 2


def _swiglu_epilogue(ys, out_refs, j):
    gate, up = ys
    out_refs[0][0] = ((gate * jax.nn.sigmoid(gate)) * up).astype(BF16)


def _mm_norm_res_kernel(a_ref, w_ref, x_ref, gate_ref, g_ref, o_ref, *rest, tl, nk, per_row, emit_copy):
    r_scr = rest[-1]
    k = pl.program_id(2)

    @pl.when(k == 0)
    def _():
        o_ref[...] = jnp.zeros_like(o_ref)

    if emit_copy:
        rest[0][...] = w_ref[...].astype(BF16)
        w_ref = rest[0]
    a = a_ref[0]
    cw = min(ACC_COLS, o_ref.shape[2])
    for c in range(o_ref.shape[2] // cw):
        cols = slice(c * cw, (c + 1) * cw)
        o_ref[0, :, cols] += _dot(a, w_ref[:, cols].astype(BF16))

    @pl.when(k == nk - 1)
    def _():
        _store_inv_rms(lambda rows: o_ref[0, rows, :], r_scr, tl)

        def finish(rows):
            gate = gate_ref[0, rows, :] if per_row else gate_ref[0]
            gain = gate * g_ref[0]
            o_ref[0, rows, :] = x_ref[0, rows, :] + (o_ref[0, rows, :] * r_scr[rows, :1]) * gain

        _for_row_chunks(tl, finish)


def _k_tile(K, D, itemsize):
    limit = max(W_BLOCK_BYTES // (D * itemsize), LANES)
    return max(t for t in range(LANES, K + 1, LANES) if K % t == 0 and t <= limit)


def _mm_norm_res(a, w, layer, x, gate, g, tl=1024, emit_copy=False):
    nb, L, K = a.shape
    D = w.shape[2]
    tl = min(tl, L)
    per_row = gate.shape[1] != 1
    r = tl if per_row else 1
    tk = _k_tile(K, D, w.dtype.itemsize)
    nk = K // tk
    out_specs = [pl.BlockSpec((1, tl, D), lambda b, i, k: (b, i, 0))]
    out_shape = [jax.ShapeDtypeStruct((nb, L, D), F32)]
    if emit_copy:
        assert nb == 1 and L == tl
        out_specs.append(pl.BlockSpec((None, tk, D), lambda b, i, k: (0, k, 0)))
        out_shape.append(jax.ShapeDtypeStruct((1, K, D), BF16))
    outs = pl.pallas_call(
        functools.partial(_mm_norm_res_kernel, tl=tl, nk=nk, per_row=per_row, emit_copy=emit_copy),
        grid=(nb, L // tl, nk),
        in_specs=[
            pl.BlockSpec((1, tl, tk), lambda b, i, k: (b, i, k)),
            pl.BlockSpec((None, tk, D), lambda b, i, k: (layer, k, 0)),
            pl.BlockSpec((1, tl, D), lambda b, i, k: (b, i, 0)),
            pl.BlockSpec((1, r, D), lambda b, i, k: (b, i if per_row else 0, 0)),
            pl.BlockSpec((1, 1, D), lambda b, i, k: (0, 0, 0)),
        ],
        out_specs=out_specs,
        out_shape=out_shape,
        scratch_shapes=[pltpu.VMEM((tl, LANES), F32)],
        compiler_params=_params(("parallel", "parallel", "arbitrary")),
        name="matmul_norm_residual",
    )(a, w, x, gate, g.reshape(1, 1, D))
    return tuple(outs) if emit_copy else outs[0]


def _scan_rows(x, op, row):
    d = 1
    while d < x.shape[0]:
        shifted = pltpu.roll(x, d, axis=0)
        x = jnp.where(row >= d, op(x, shifted), x)
        d *= 2
    return x


def _mlstm_kernel(qk_ref, v_ref, og_ref, gt_ref, bias_ref, gh_ref, c0_ref, n0_ref, m0_ref,
                  o_ref, c_out_ref, n_out_ref, m_out_ref, c_scr, n_scr, m_scr,
                  *, valid_len, nc, dqk, dv):
    c = pl.program_id(1)
    L = MLSTM_CHUNK

    @pl.when(c == 0)
    def _():
        c_scr[...] = c0_ref[0]
        n_scr[...] = n0_ref[0]
        m_scr[...] = m0_ref[0]

    gates = gt_ref[0] + bias_ref[...]
    li = gates[:, :LANES]
    gf = gates[:, LANES:]
    lf = jnp.minimum(gf, 0.0) - jnp.log1p(jnp.exp(-jnp.abs(gf)))
    row = lax.broadcasted_iota(jnp.int32, (L, LANES), 0)
    col = lax.broadcasted_iota(jnp.int32, (L, LANES), 1)
    if valid_len < L:
        li = jnp.where(row < valid_len, li, NEG_BIG)
        lf = jnp.where(row < valid_len, lf, 0.0)

    b = _scan_rows(lf, jnp.add, row)
    u = li - b
    m_prev = m_scr[...]
    m_t = b + jnp.maximum(m_prev, _scan_rows(u, jnp.maximum, row))
    d_inter = jnp.exp(b + m_prev - m_t)
    ct = b - m_t
    inv_floor = jnp.exp(-m_t)
    b_end = b[L - 1:L, :]
    m_end = m_t[L - 1:L, :]
    w_all = jnp.exp(b_end - b + li - m_end)
    decay = jnp.exp(b_end + m_prev - m_end)
    u_t = u.T
    causal = row >= col

    for h in range(MH):
        q = qk_ref[0, :, h * dqk:(h + 1) * dqk]
        k = qk_ref[0, :, MH * dqk + h * dqk:MH * dqk + (h + 1) * dqk]
        v = v_ref[0, :, h * dv:(h + 1) * dv]
        log_d = jnp.where(causal, ct[:, h:h + 1] + u_t[h:h + 1, :], NEG_BIG)
        s = _dot_nt(q, k) * jnp.exp(log_d)
        di = d_inter[:, h:h + 1]
        c_prev = c_scr[h]
        num = _dot(s.astype(BF16), v) + di * _dot(q, c_prev.astype(BF16))
        n_prev = n_scr[h:h + 1, :]
        qn_state = jnp.sum(q.astype(F32) * n_prev.astype(BF16).astype(F32), axis=-1, keepdims=True)
        qn = jnp.sum(s, axis=-1, keepdims=True) + di * qn_state
        hh = num / jnp.maximum(jnp.abs(qn), inv_floor[:, h:h + 1])
        hh = hh * lax.rsqrt(jnp.mean(hh * hh, axis=-1, keepdims=True) + EPS)
        hh = hh * gh_ref[:, h * dv:(h + 1) * dv]
        og = og_ref[0, :, h * dv:(h + 1) * dv]
        o_ref[0, :, h * dv:(h + 1) * dv] = (hh * jax.nn.sigmoid(og)).astype(BF16)

        kw = k.astype(F32) * w_all[:, h:h + 1]
        dec = decay[:, h:h + 1]
        c_scr[h] = dec * c_prev + _dot_tn(kw.astype(BF16), v)
        n_scr[h:h + 1, :] = dec * n_prev + jnp.sum(kw, axis=0, keepdims=True)

    m_scr[...] = m_end

    @pl.when(c == nc - 1)
    def _():
        c_out_ref[0] = c_scr[...]
        n_out_ref[0] = n_scr[...]
        m_out_ref[0] = m_scr[...]


def _mlstm(qk, v, og, gates, bias_row, g_head, c0, n0, m0, valid_len):
    nb, L, vt = v.shape
    dv = vt // MH
    dqk = qk.shape[2] // (2 * MH)
    nc = L // MLSTM_CHUNK
    blk = lambda width: pl.BlockSpec((1, MLSTM_CHUNK, width), lambda b, c: (b, c, 0))
    return pl.pallas_call(
        functools.partial(_mlstm_kernel, valid_len=valid_len, nc=nc, dqk=dqk, dv=dv),
        grid=(nb, nc),
        in_specs=[
            blk(qk.shape[2]), blk(vt), blk(vt), blk(2 * LANES),
            pl.BlockSpec((1, 2 * LANES), lambda b, c: (0, 0)),
            pl.BlockSpec((1, vt), lambda b, c: (0, 0)),
            pl.BlockSpec((1, MH, dqk, dv), lambda b, c: (b, 0, 0, 0)),
            pl.BlockSpec((1, MH, dqk), lambda b, c: (b, 0, 0)),
            pl.BlockSpec((1, 1, LANES), lambda b, c: (b, 0, 0)),
        ],
        out_specs=[
            blk(vt),
            pl.BlockSpec((1, MH, dqk, dv), lambda b, c: (b, 0, 0, 0)),
            pl.BlockSpec((1, MH, dqk), lambda b, c: (b, 0, 0)),
            pl.BlockSpec((1, 1, LANES), lambda b, c: (b, 0, 0)),
        ],
        out_shape=[
            jax.ShapeDtypeStruct((nb, L, vt), BF16),
            jax.ShapeDtypeStruct((nb, MH, dqk, dv), F32),
            jax.ShapeDtypeStruct((nb, MH, dqk), F32),
            jax.ShapeDtypeStruct((nb, 1, LANES), F32),
        ],
        scratch_shapes=[
            pltpu.VMEM((MH, dqk, dv), F32),
            pltpu.VMEM((MH, dqk), F32),
            pltpu.VMEM((1, LANES), F32),
        ],
        compiler_params=_params(("parallel", "arbitrary")),
        name="mlstm_chunks",
    )(qk, v, og, gates, bias_row, g_head.reshape(1, vt), c0, n0, m0)


def _suffix_matrix():
    j = lax.broadcasted_iota(jnp.int32, (2 * SB_BLOCK, 2 * SB_BLOCK), 0) % SB_BLOCK
    s = lax.broadcasted_iota(jnp.int32, (2 * SB_BLOCK, 2 * SB_BLOCK), 1)
    return jnp.where((s >= SB_BLOCK) | (j > s), 1.0, 0.0).astype(BF16)


def _sb_scores(z, mask, suffix_mat):
    soft = jnp.log2(1.0 + jnp.exp2(-jnp.abs(z)))
    log_beta = jnp.minimum(z, 0.0) - soft
    log_1mb = log_beta - z
    if mask is not None:
        log_1mb = jnp.where(mask, log_1mb, 0.0)
    hi = log_1mb.astype(BF16)
    lo = (log_1mb - hi.astype(F32)).astype(BF16)
    sums = _dot(jnp.concatenate([hi, lo], axis=1), suffix_mat)
    return log_beta, sums[:, :SB_BLOCK], sums[:, SB_BLOCK:]


def _sb_visit(zs, masks, value_fns, carry, suffix_mat):
    scores = [_sb_scores(z, m, suffix_mat) for z, m in zip(zs, masks)]
    pv = None
    for (log_beta, later, total), mask, value_fn in zip(scores, masks, value_fns):
        a = jnp.exp2(log_beta + later + carry)
        if mask is not None:
            a = jnp.where(mask, a, 0.0)
        out = value_fn(a.astype(BF16))
        pv = out if pv is None else pv + out
        carry = carry + total
    return pv, carry


def _sb_prompt_kernel(q_ref, k_ref, v_ref, bias_ref, o_ref, k_scr, v_scr, acc_scr, carry_scr,
                      *, n_qt, scale):
    k_scr[...] = k_ref[0].astype(BF16)
    v_scr[...] = v_ref[0].astype(BF16)
    suffix_mat = _suffix_matrix()
    bias = bias_ref[...] * LOG2E
    sub = SB_QTILE // SB_BLOCK

    def visit(r0, r1, blocks, mask):
        q = q_ref[0, r0:r1, :]
        zs, value_fns = [], []
        for j in blocks:
            keys = pl.ds(pl.multiple_of(j * SB_BLOCK, SB_BLOCK), SB_BLOCK)
            zs.append(_dot_nt(q, k_scr[keys, :]) * (scale * LOG2E) + bias)
            value_fns.append(lambda a, keys=keys: _dot(a, v_scr[keys, :]))
        pv, carry = _sb_visit(zs, [mask] * len(zs), value_fns, carry_scr[r0:r1, :], suffix_mat)
        acc_scr[r0:r1, :] += pv
        carry_scr[r0:r1, :] = carry

    acc_scr[...] = jnp.zeros_like(acc_scr)
    carry_scr[...] = jnp.zeros_like(carry_scr)
    for c in reversed(range(sub)):
        rows_left = SB_QTILE - c * SB_BLOCK
        row = lax.broadcasted_iota(jnp.int32, (rows_left, SB_BLOCK), 0)
        col = lax.broadcasted_iota(jnp.int32, (rows_left, SB_BLOCK), 1)
        for t in range(n_qt):
            visit(t * SB_QTILE + c * SB_BLOCK, (t + 1) * SB_QTILE, [t * sub + c], col < row)

    for t in range(1, n_qt):
        def older(i, _, t=t):
            newest = (t - i) * sub - 1
            visit(t * SB_QTILE, (t + 1) * SB_QTILE, [newest - d for d in range(sub)], None)
            return 0

        lax.fori_loop(0, t, older, 0)
    o_ref[0] = acc_scr[...].astype(BF16)


def _sb_prompt(q, k, v, bias_lanes):
    nb, L, hd = q.shape
    dh = hd // SB_HEADS
    spec = pl.BlockSpec((1, L, dh), lambda b, h: (b, 0, h))
    return pl.pallas_call(
        functools.partial(_sb_prompt_kernel, n_qt=L // SB_QTILE, scale=dh ** -0.5),
        grid=(nb, SB_HEADS),
        in_specs=[spec, spec, spec, pl.BlockSpec((1, dh), lambda b, h: (0, h))],
        out_specs=spec,
        out_shape=jax.ShapeDtypeStruct((nb, L, hd), BF16),
        scratch_shapes=[pltpu.VMEM((L, dh), BF16), pltpu.VMEM((L, dh), BF16),
                        pltpu.VMEM((L, dh), F32), pltpu.VMEM((L, SB_BLOCK), F32)],
        compiler_params=_params(("parallel", "parallel")),
        name="stickbreak_prompt",
    )(q, k, v, bias_lanes)


def _sb_decode_kernel(pt_ref, q_ref, bias_ref, kn_ref, vn_ref, *rest, n_steps, n_new, scale):
    page_refs = rest[:2 * DECODE_PAGES]
    o_ref, carry_scr, acc_scr, qbd_scr, expand_scr, hmask_scr = rest[2 * DECODE_PAGES:]
    p = pl.program_id(1)
    rows = SB_HEADS * n_new
    dh = q_ref.shape[2] // SB_HEADS
    suffix_mat = _suffix_matrix()

    @pl.when(p == 0)
    def _():
        carry_scr[...] = jnp.zeros_like(carry_scr)
        acc_scr[...] = jnp.zeros_like(acc_scr)
        q_rep = jnp.concatenate([q_ref[0]] * SB_HEADS, axis=0)
        lane_head = lax.broadcasted_iota(jnp.int32, q_rep.shape, 1) // dh
        row_head = lax.broadcasted_iota(jnp.int32, q_rep.shape, 0) // n_new
        qbd_scr[...] = jnp.where(lane_head == row_head, q_rep, 0.0).astype(BF16)
        tok = lax.broadcasted_iota(jnp.int32, expand_scr.shape, 0)
        c = lax.broadcasted_iota(jnp.int32, expand_scr.shape, 1)
        expand_scr[...] = jnp.where(c // SB_HEADS == tok, 1.0, 0.0).astype(BF16)
        r = lax.broadcasted_iota(jnp.int32, hmask_scr.shape, 0)
        c = lax.broadcasted_iota(jnp.int32, hmask_scr.shape, 1)
        hmask_scr[...] = jnp.where(c % SB_HEADS == r // n_new, 1.0, 0.0).astype(BF16)

    def attend(kv_refs, mask):
        zs, value_fns = [], []
        for k_ref, v_ref in kv_refs:
            k_all = jnp.concatenate(
                [k_ref[pl.ds(h, PAGE_SIZE, stride=SB_HEADS), :].astype(BF16) for h in range(SB_HEADS)],
                axis=1)
            zs.append(_dot_nt(qbd_scr[...], k_all) * (scale * LOG2E) + bias_ref[...] * LOG2E)

            def values(a, v_ref=v_ref):
                spread = _dot(a, expand_scr[...]).astype(BF16) * hmask_scr[...]
                return _dot(spread, v_ref[...].astype(BF16))

            value_fns.append(values)
        pv, carry = _sb_visit(zs, [mask] * len(zs), value_fns, carry_scr[...], suffix_mat)
        acc_scr[...] += pv
        carry_scr[...] = carry

    @pl.when(p == 0)
    def _():
        row = lax.broadcasted_iota(jnp.int32, (rows, SB_BLOCK), 0)
        col = lax.broadcasted_iota(jnp.int32, (rows, SB_BLOCK), 1)
        attend([(kn_ref, vn_ref)], col < row % n_new)

    @pl.when(p > 0)
    def _():
        attend([(page_refs[2 * i], page_refs[2 * i + 1]) for i in range(DECODE_PAGES)], None)

    @pl.when(p == n_steps - 1)
    def _():
        o_ref[0] = acc_scr[...]


def _sb_decode(page_table, q, bias_rows, k_new, v_new, pool_k, pool_v, layer):
    nb, n_new, hd = q.shape
    dh = hd // SB_HEADS
    n_pages = page_table.shape[1]
    rows = SB_HEADS * n_new
    prow = PAGE_SIZE * SB_HEADS
    assert n_pages % DECODE_PAGES == 0 and rows == SB_BLOCK and PAGE_SIZE == SB_BLOCK
    n_steps = n_pages // DECODE_PAGES + 1
    n_layers, n_pool = pool_v.shape[:2]
    pool_k = pool_k.reshape(n_layers, n_pool, prow, dh)
    pool_v = pool_v.reshape(n_layers, n_pool, prow, dh)
    k_new = k_new.reshape(nb, prow, dh)
    v_new = v_new.reshape(nb, prow, dh)

    def page_spec(slot):
        return pl.BlockSpec(
            (None, None, prow, dh),
            lambda b, p, pt: (layer, pt[b, n_pages - DECODE_PAGES * jnp.maximum(p, 1) + DECODE_PAGES - 1 - slot], 0, 0))

    new_spec = pl.BlockSpec((None, prow, dh), lambda b, p, pt: (b, 0, 0))
    grid_spec = pltpu.PrefetchScalarGridSpec(
        num_scalar_prefetch=1,
        grid=(nb, n_steps),
        in_specs=[
            pl.BlockSpec((1, n_new, hd), lambda b, p, pt: (b, 0, 0)),
            pl.BlockSpec((rows, SB_BLOCK), lambda b, p, pt: (0, 0)),
            new_spec, new_spec,
        ] + [page_spec(slot) for slot in range(DECODE_PAGES) for _ in range(2)],
        out_specs=pl.BlockSpec((1, rows, dh), lambda b, p, pt: (b, 0, 0)),
        scratch_shapes=[pltpu.VMEM((rows, SB_BLOCK), F32), pltpu.VMEM((rows, dh), F32),
                        pltpu.VMEM((rows, hd), BF16), pltpu.VMEM((PAGE_SIZE, prow), BF16),
                        pltpu.VMEM((rows, prow), BF16)],
    )
    return pl.pallas_call(
        functools.partial(_sb_decode_kernel, n_steps=n_steps, n_new=n_new, scale=dh ** -0.5),
        grid_spec=grid_spec,
        out_shape=jax.ShapeDtypeStruct((nb, rows, dh), F32),
        compiler_params=_params(("parallel", "arbitrary")),
        name="stickbreak_decode",
    )(page_table, q, bias_rows, k_new, v_new, *([pool_k, pool_v] * DECODE_PAGES))


def _ffn(x, mods, g_pre, g_post, w_in, layer_in, w_out, layer_out, tf=512, emit_copy=False):
    d_ff = w_out.shape[1]
    (act,) = _modmm(x, g_pre, mods[3], mods[4], w_in, layer_in, (0, d_ff), tf, d_ff // tf,
                    (BF16,), _swiglu_epilogue)
    return _mm_norm_res(act, w_out, layer_out, x, mods[5], g_post, emit_copy=emit_copy)


def _mlstm_gate_operands(w_in, b_gate):
    D = w_in.shape[0]
    first = w_in.shape[1] - 2 * MH
    w_gate = jnp.zeros((D, 2 * LANES), F32)
    w_gate = w_gate.at[:, :MH].set(w_in[:, first:first + MH])
    w_gate = w_gate.at[:, LANES:LANES + MH].set(w_in[:, first + MH:])
    bias_row = jnp.zeros((1, 2 * LANES), F32)
    bias_row = bias_row.at[0, :MH].set(b_gate[0].astype(F32))
    bias_row = bias_row.at[0, LANES:LANES + MH].set(b_gate[1].astype(F32))
    return w_gate, bias_row


def _mlstm_layer(x, mods, g_norm, w_in, w_out, layer, w_gate, bias_row, g_head, c0, n0, m0,
                 chunk_rows, emit_copy=False):
    dqk = c0.shape[2]
    qk_tot = 2 * MH * dqk
    tn = _proj_tile(x.shape[1], qk_tot, w_in.dtype.itemsize)
    epilogue = [functools.partial(_scaled_q_then_k, q_cols=MH * dqk, q_scale=dqk ** -0.5),
                _as_bf16, _as_f32]
    proj = _modmm(x, g_norm[0], mods[0], mods[1], w_in, layer, (0, qk_tot, 2 * qk_tot), tn,
                  qk_tot // tn, (BF16, BF16, F32), epilogue, w_gate=w_gate, emit_copy=emit_copy)
    qk, v, og, gates = proj[:4]
    m0p = jnp.zeros((m0.shape[0], 1, LANES), F32).at[:, 0, :MH].set(m0.astype(F32))
    if chunk_rows == MLSTM_CHUNK:
        h, c_new, n_new, m_new = _mlstm(qk, v, og, gates, bias_row, g_head,
                                        c0.astype(F32), n0.astype(F32), m0p, MLSTM_CHUNK)
    else:
        nb = x.shape[1] // chunk_rows

        def pad(a):
            a = a.reshape(nb, chunk_rows, a.shape[2])
            return jnp.pad(a, ((0, 0), (0, MLSTM_CHUNK - chunk_rows), (0, 0)))

        h, c_new, n_new, m_new = _mlstm(pad(qk), pad(v), pad(og), pad(gates), bias_row, g_head,
                                        c0.astype(F32), n0.astype(F32), m0p, chunk_rows)
        h = h[:, :chunk_rows].reshape(1, nb * chunk_rows, h.shape[2])
    res = _mm_norm_res(h, w_out, layer, x, mods[2], g_norm[1], emit_copy=emit_copy)
    if emit_copy:
        return res[0], c_new, n_new, m_new[:, 0, :MH], (proj[4], res[1])
    return res, c_new, n_new, m_new[:, 0, :MH], None


def kernel(x_prompt, x_sample, state_C, state_n, state_m, cache_k, cache_v, page_table, c_prompt, c_sample, w_ada, b_ada, g_norm, w_in_a, b_gate_a, g_head_a, w_out_a, w_in_b, b_sb, w_out_b, w_ffn_in, w_ffn_out):
    bp, seq, D = x_prompt.shape
    bs, dec_seq, _ = x_sample.shape
    depth = w_ada.shape[0]
    dh = D // SB_HEADS

    rows = -(-(bp + bs) // 8) * 8
    c_all = jnp.zeros((rows, D), F32).at[:bp].set(c_prompt).at[bp:bp + bs].set(c_sample)
    ada = _ada(c_all, w_ada, b_ada)

    xp = x_prompt
    xs = x_sample.reshape(1, bs * dec_seq, D)
    outs = {name: [] for name in ("kp", "vp", "ks", "vs", "Cp", "np", "mp", "Cs", "ns", "ms")}
    for i in range(depth):
        j = i // 2
        mods_p = [ada[i, :bp, s * D:(s + 1) * D].reshape(bp, 1, D) for s in range(6)]
        mods_s = [jnp.repeat(ada[i, bp:bp + bs, s * D:(s + 1) * D], dec_seq, axis=0).reshape(1, bs * dec_seq, D)
                  for s in range(6)]
        if i % 2 == 0:
            w_gate, bias_row = _mlstm_gate_operands(w_in_a[j], b_gate_a[j])
            xs, Cs, n_s, m_s, (w_in_c, w_out_c) = _mlstm_layer(
                xs, mods_s, g_norm[i], w_in_a, w_out_a, j, w_gate, bias_row, g_head_a[j],
                state_C[j], state_n[j], state_m[j], dec_seq, emit_copy=True)
            zc = jnp.zeros((bp,) + state_C.shape[2:], F32)
            zn = jnp.zeros((bp,) + state_n.shape[2:], F32)
            zm = jnp.zeros((bp,) + state_m.shape[2:], F32)
            xp, Cp, n_p, m_p, _ = _mlstm_layer(
                xp, mods_p, g_norm[i], w_in_c, w_out_c, 0, w_gate, bias_row, g_head_a[j],
                zc, zn, zm, MLSTM_CHUNK)
            outs["Cp"].append(Cp); outs["np"].append(n_p); outs["mp"].append(m_p)
            outs["Cs"].append(Cs); outs["ns"].append(n_s); outs["ms"].append(m_s)
        else:
            bias = b_sb[j].astype(F32)
            tn = _proj_tile(xs.shape[1], D, w_in_b.dtype.itemsize)
            qs, kn, vn, w_in_c = _modmm(xs, g_norm[i, 0], mods_s[0], mods_s[1], w_in_b, j, (0, D, 2 * D), tn,
                                        D // tn, (F32, F32, F32), [_as_f32, _as_f32, _as_f32], emit_copy=True)

            def as_page(a):
                a = a.reshape(bs, dec_seq, SB_HEADS, dh)
                return jnp.pad(a, ((0, 0), (0, PAGE_SIZE - dec_seq), (0, 0), (0, 0)))

            os_ = _sb_decode(page_table, qs.reshape(bs, dec_seq, D),
                             jnp.broadcast_to(jnp.repeat(bias, dec_seq)[:, None], (SB_HEADS * dec_seq, SB_BLOCK)),
                             as_page(kn), as_page(vn), cache_k, cache_v, j)
            os_ = os_.reshape(bs, SB_HEADS, dec_seq, dh).transpose(0, 2, 1, 3).reshape(1, bs * dec_seq, D)
            xs, w_out_c = _mm_norm_res(os_.astype(BF16), w_out_b, j, xs, mods_s[2], g_norm[i, 1], emit_copy=True)
            tn = _proj_tile(seq, D, w_in_c.dtype.itemsize)
            qp, kp, vp = _modmm(xp, g_norm[i, 0], mods_p[0], mods_p[1], w_in_c, 0, (0, D, 2 * D), tn,
                                D // tn, (BF16, F32, F32), [_as_bf16, _as_f32, _as_f32])
            op = _sb_prompt(qp, kp, vp, jnp.repeat(bias, dh).reshape(1, D))
            xp = _mm_norm_res(op, w_out_c, 0, xp, mods_p[2], g_norm[i, 1])
            outs["kp"].append(kp.reshape(bp, seq, SB_HEADS, dh)); outs["vp"].append(vp.reshape(bp, seq, SB_HEADS, dh))
            outs["ks"].append(kn.reshape(bs, dec_seq, SB_HEADS, dh)); outs["vs"].append(vn.reshape(bs, dec_seq, SB_HEADS, dh))
        xs, w_fo_c = _ffn(xs, mods_s, g_norm[i, 2], g_norm[i, 3], w_ffn_in, i, w_ffn_out, i, emit_copy=True)
        xp = _ffn(xp, mods_p, g_norm[i, 2], g_norm[i, 3], w_ffn_in, i, w_fo_c, 0)

    st = lambda name: jnp.stack(outs[name])
    return (xp, xs.reshape(bs, dec_seq, D), st("kp"), st("vp"), st("ks"), st("vs"),
            st("Cp"), st("np"), st("mp"), st("Cs"), st("ns"), st("ms"))
```

```python
import functools

import jax
import jax.numpy as jnp
from jax import lax
from jax.experimental import pallas as pl
from jax.experimental.pallas import tpu as pltpu

F32 = jnp.float32
BF16 = jnp.bfloat16

EPS = 1e-6
MH = 8
MLSTM_CHUNK = 128
SB_HEADS = 16
SB_BLOCK = 128
SB_QTILE = 512
SB_OLDER_GROUP = 8
PAGE_SIZE = 128
NEG_BIG = -1e30
LOG2E = 1.4426950408889634

LANES = 128
SUBLANES = 8
VMEM_LIMIT = 52 * 1024 * 1024
ACC_COLS = 512
PROJ_TILE = 512
DECODE_PAGES = 8
W_BLOCK_BYTES = 6 * 1024 * 1024


def _dot(a, b):
    return jnp.dot(a, b, preferred_element_type=F32)


def _dot_nt(a, b):
    return lax.dot_general(a, b, (((1,), (1,)), ((), ())), preferred_element_type=F32)


def _dot_tn(a, b):
    return lax.dot_general(a, b, (((0,), (0,)), ((), ())), preferred_element_type=F32)


def _params(sem):
    return pltpu.CompilerParams(dimension_semantics=sem, vmem_limit_bytes=VMEM_LIMIT)


def _ada_kernel(c_ref, w_ref, b_ref, o_ref):
    c = c_ref[...]
    a = (c * jax.nn.sigmoid(c)).astype(BF16)
    o_ref[0] = _dot(a, w_ref[0].astype(BF16)) + b_ref[0]


def _ada(c_all, w_ada, b_ada, tn=1024):
    depth, d, n = w_ada.shape
    rows = c_all.shape[0]
    return pl.pallas_call(
        _ada_kernel,
        grid=(depth, n // tn),
        in_specs=[
            pl.BlockSpec((rows, d), lambda l, j: (0, 0)),
            pl.BlockSpec((1, d, tn), lambda l, j: (l, 0, j)),
            pl.BlockSpec((1, 1, tn), lambda l, j: (l, 0, j)),
        ],
        out_specs=pl.BlockSpec((1, rows, tn), lambda l, j: (l, 0, j)),
        out_shape=jax.ShapeDtypeStruct((depth, rows, n), F32),
        compiler_params=_params(("parallel", "parallel")),
        name="ada_params",
    )(c_all, w_ada, b_ada.reshape(depth, 1, n))


def _row_chunks(tl):
    rc = min(tl, 256)
    return rc, tl // rc


def _for_row_chunks(tl, body):
    rc, n_chunks = _row_chunks(tl)
    if n_chunks == 1:
        body(pl.ds(0, rc))
    else:
        pl.loop(0, n_chunks)(lambda c: body(pl.ds(pl.multiple_of(c * rc, rc), rc)))


def _store_inv_rms(read_rows, r_scr, tl):
    def body(rows):
        v = read_rows(rows)
        r = lax.rsqrt(jnp.mean(v * v, axis=-1, keepdims=True) + EPS)
        r_scr[rows, :] = jnp.broadcast_to(r, (r.shape[0], LANES))

    _for_row_chunks(tl, body)


def _modmm_kernel(*refs, n_slices, n_tiles, epilogue, sequential, has_gate, emit_copy, tl, per_row):
    x_ref, g_ref, sh_ref, sc_ref = refs[:4]
    n_w = 1 if sequential else n_slices
    w_refs = refs[4:4 + n_w]
    pos = 4 + n_w
    if has_gate:
        wg_ref = refs[pos]
        pos += 1
    n_out = len(refs) - pos - 2 - (1 if has_gate else 0) - (1 if emit_copy else 0)
    out_refs = refs[pos:pos + n_out]
    pos += n_out
    if has_gate:
        go_ref = refs[pos]
        pos += 1
    if emit_copy:
        wcopy_ref = refs[pos]
        pos += 1
    h_scr, r_scr = refs[pos], refs[pos + 1]
    j = pl.program_id(2)

    @pl.when(j == 0)
    def _():
        _store_inv_rms(lambda rows: x_ref[0, rows, :], r_scr, tl)

        def modulate(rows):
            if per_row:
                sc, sh = sc_ref[0, rows, :], sh_ref[0, rows, :]
            else:
                sc, sh = sc_ref[0], sh_ref[0]
            gain = g_ref[0] * (1.0 + sc)
            h = (x_ref[0, rows, :] * r_scr[rows, :1]) * gain + sh
            h_scr[rows, :] = h.astype(BF16)

        _for_row_chunks(tl, modulate)
        if has_gate:
            go_ref[0] = _dot(h_scr[...], wg_ref[...].astype(BF16))

    if sequential:
        if emit_copy:
            wcopy_ref[...] = w_refs[0][...].astype(BF16)
            rhs_ref = wcopy_ref
        else:
            rhs_ref = w_refs[0]
        for s, out_ref in enumerate(out_refs):
            @pl.when(j // n_tiles == s)
            def _(s=s, out_ref=out_ref):
                y = _dot(h_scr[...], rhs_ref[...].astype(BF16))
                out_ref[0] = epilogue[s](y, j - s * n_tiles)
    else:
        hb = h_scr[...]
        ys = [_dot(hb, w_ref[...].astype(BF16)) for w_ref in w_refs]
        epilogue(ys, out_refs, j)


def _modmm(x, g, shift, scale, w, layer, slice_starts, tn, n_tiles, out_dtypes, epilogue,
           w_gate=None, tl=1024, emit_copy=False):
    nb, L, D = x.shape
    tl = min(tl, L)
    per_row = shift.shape[1] != 1
    r = tl if per_row else 1
    n_slices = len(slice_starts)
    has_gate = w_gate is not None
    sequential = isinstance(epilogue, (list, tuple))
    if sequential:
        assert all(s0 == slice_starts[0] + s * n_tiles * tn for s, s0 in enumerate(slice_starts))

    def mod_map(b, i, j):
        return (b, i if per_row else 0, 0)

    in_specs = [
        pl.BlockSpec((1, tl, D), lambda b, i, j: (b, i, 0)),
        pl.BlockSpec((1, 1, D), lambda b, i, j: (0, 0, 0)),
        pl.BlockSpec((1, r, D), mod_map),
        pl.BlockSpec((1, r, D), mod_map),
    ]
    args = [x, g.reshape(1, 1, D), shift, scale]
    for s0 in slice_starts[:1] if sequential else slice_starts:
        off = s0 // tn
        in_specs.append(pl.BlockSpec((None, D, tn), lambda b, i, j, off=off: (layer, 0, off + j)))
        args.append(w)
    if has_gate:
        gw = w_gate.shape[1]
        in_specs.append(pl.BlockSpec((D, gw), lambda b, i, j: (0, 0)))
        args.append(w_gate)
    if sequential:
        out_specs = [
            pl.BlockSpec((1, tl, tn), lambda b, i, j, s=s: (b, i, jnp.clip(j - s * n_tiles, 0, n_tiles - 1)))
            for s in range(n_slices)]
    else:
        out_specs = [pl.BlockSpec((1, tl, tn), lambda b, i, j: (b, i, j)) for _ in out_dtypes]
    out_shape = [jax.ShapeDtypeStruct((nb, L, n_tiles * tn), dt) for dt in out_dtypes]
    if has_gate:
        out_specs.append(pl.BlockSpec((1, tl, gw), lambda b, i, j: (b, i, 0)))
        out_shape.append(jax.ShapeDtypeStruct((nb, L, gw), F32))
    if emit_copy:
        assert sequential and nb == 1 and L == tl and slice_starts[0] == 0
        out_specs.append(pl.BlockSpec((None, D, tn), lambda b, i, j: (0, 0, j)))
        out_shape.append(jax.ShapeDtypeStruct((1, D, n_slices * n_tiles * tn), BF16))
    return pl.pallas_call(
        functools.partial(_modmm_kernel, n_slices=n_slices, n_tiles=n_tiles, epilogue=epilogue,
                          sequential=sequential, has_gate=has_gate, emit_copy=emit_copy, tl=tl,
                          per_row=per_row),
        grid=(nb, L // tl, n_tiles * n_slices if sequential else n_tiles),
        in_specs=in_specs,
        out_specs=out_specs,
        out_shape=out_shape,
        scratch_shapes=[pltpu.VMEM((tl, D), BF16), pltpu.VMEM((tl, LANES), F32)],
        compiler_params=_params(("parallel", "parallel", "arbitrary")),
        name="modulated_matmul",
    )(*args)


def _as_bf16(y, tile):
    return y.astype(BF16)


def _as_f32(y, tile):
    return y


def _scaled_q_then_k(y, tile, *, q_cols, q_scale):
    tn = y.shape[1]
    if q_cols % tn == 0:
        factor = jnp.where(tile < q_cols // tn, q_scale, 1.0).astype(F32)
    else:
        col = tile * tn + lax.broadcasted_iota(jnp.int32, (1, tn), 1)
        factor = jnp.where(col < q_cols, q_scale, 1.0).astype(F32)
    return (y * factor).astype(BF16)


def _proj_tile(rows, width, itemsize):
    if rows >= 2 * PROJ_TILE:
        return PROJ_TILE
    return width if itemsize <= 2 else width // 2


def _swiglu_epilogue(ys, out_refs, j):
    gate, up = ys
    out_refs[0][0] = ((gate * jax.nn.sigmoid(gate)) * up).astype(BF16)


def _mm_norm_res_kernel(a_ref, w_ref, x_ref, gate_ref, g_ref, o_ref, *rest, tl, nk, per_row, emit_copy):
    r_scr = rest[-1]
    k = pl.program_id(2)

    @pl.when(k == 0)
    def _():
        o_ref[...] = jnp.zeros_like(o_ref)

    if emit_copy:
        rest[0][...] = w_ref[...].astype(BF16)
        w_ref = rest[0]
    a = a_ref[0]
    cw = min(ACC_COLS, o_ref.shape[2])
    for c in range(o_ref.shape[2] // cw):
        cols = slice(c * cw, (c + 1) * cw)
        o_ref[0, :, cols] += _dot(a, w_ref[:, cols].astype(BF16))

    @pl.when(k == nk - 1)
    def _():
        _store_inv_rms(lambda rows: o_ref[0, rows, :], r_scr, tl)

        def finish(rows):
            gate = gate_ref[0, rows, :] if per_row else gate_ref[0]
            gain = gate * g_ref[0]
            o_ref[0, rows, :] = x_ref[0, rows, :] + (o_ref[0, rows, :] * r_scr[rows, :1]) * gain

        _for_row_chunks(tl, finish)


def _k_tile(K, D, itemsize):
    limit = max(W_BLOCK_BYTES // (D * itemsize), LANES)
    return max(t for t in range(LANES, K + 1, LANES) if K % t == 0 and t <= limit)


def _mm_norm_res(a, w, layer, x, gate, g, tl=1024, emit_copy=False):
    nb, L, K = a.shape
    D = w.shape[2]
    tl = min(tl, L)
    per_row = gate.shape[1] != 1
    r = tl if per_row else 1
    tk = _k_tile(K, D, w.dtype.itemsize)
    nk = K // tk
    out_specs = [pl.BlockSpec((1, tl, D), lambda b, i, k: (b, i, 0))]
    out_shape = [jax.ShapeDtypeStruct((nb, L, D), F32)]
    if emit_copy:
        assert nb == 1 and L == tl
        out_specs.append(pl.BlockSpec((None, tk, D), lambda b, i, k: (0, k, 0)))
        out_shape.append(jax.ShapeDtypeStruct((1, K, D), BF16))
    outs = pl.pallas_call(
        functools.partial(_mm_norm_res_kernel, tl=tl, nk=nk, per_row=per_row, emit_copy=emit_copy),
        grid=(nb, L // tl, nk),
        in_specs=[
            pl.BlockSpec((1, tl, tk), lambda b, i, k: (b, i, k)),
            pl.BlockSpec((None, tk, D), lambda b, i, k: (layer, k, 0)),
            pl.BlockSpec((1, tl, D), lambda b, i, k: (b, i, 0)),
            pl.BlockSpec((1, r, D), lambda b, i, k: (b, i if per_row else 0, 0)),
            pl.BlockSpec((1, 1, D), lambda b, i, k: (0, 0, 0)),
        ],
        out_specs=out_specs,
        out_shape=out_shape,
        scratch_shapes=[pltpu.VMEM((tl, LANES), F32)],
        compiler_params=_params(("parallel", "parallel", "arbitrary")),
        name="matmul_norm_residual",
    )(a, w, x, gate, g.reshape(1, 1, D))
    return tuple(outs) if emit_copy else outs[0]


def _scan_rows(x, op, row):
    d = 1
    while d < x.shape[0]:
        shifted = pltpu.roll(x, d, axis=0)
        x = jnp.where(row >= d, op(x, shifted), x)
        d *= 2
    return x


def _mlstm_kernel(qk_ref, v_ref, og_ref, gt_ref, bias_ref, gh_ref, c0_ref, n0_ref, m0_ref,
                  o_ref, c_out_ref, n_out_ref, m_out_ref, c_scr, n_scr, m_scr,
                  *, valid_len, nc, dqk, dv):
    c = pl.program_id(1)
    L = MLSTM_CHUNK

    @pl.when(c == 0)
    def _():
        c_scr[...] = c0_ref[0]
        n_scr[...] = n0_ref[0]
        m_scr[...] = m0_ref[0]

    gates = gt_ref[0] + bias_ref[...]
    li = gates[:, :LANES]
    gf = gates[:, LANES:]
    lf = jnp.minimum(gf, 0.0) - jnp.log1p(jnp.exp(-jnp.abs(gf)))
    row = lax.broadcasted_iota(jnp.int32, (L, LANES), 0)
    col = lax.broadcasted_iota(jnp.int32, (L, LANES), 1)
    if valid_len < L:
        li = jnp.where(row < valid_len, li, NEG_BIG)
        lf = jnp.where(row < valid_len, lf, 0.0)

    b = _scan_rows(lf, jnp.add, row)
    u = li - b
    m_prev = m_scr[...]
    m_t = b + jnp.maximum(m_prev, _scan_rows(u, jnp.maximum, row))
    d_inter = jnp.exp(b + m_prev - m_t)
    ct = b - m_t
    inv_floor = jnp.exp(-m_t)
    b_end = b[L - 1:L, :]
    m_end = m_t[L - 1:L, :]
    w_all = jnp.exp(b_end - b + li - m_end)
    decay = jnp.exp(b_end + m_prev - m_end)
    u_t = u.T
    causal = row >= col

    for h in range(MH):
        q = qk_ref[0, :, h * dqk:(h + 1) * dqk]
        k = qk_ref[0, :, MH * dqk + h * dqk:MH * dqk + (h + 1) * dqk]
        v = v_ref[0, :, h * dv:(h + 1) * dv]
        log_d = jnp.where(causal, ct[:, h:h + 1] + u_t[h:h + 1, :], NEG_BIG)
        s = _dot_nt(q, k) * jnp.exp(log_d)
        di = d_inter[:, h:h + 1]
        c_prev = c_scr[h]
        num = _dot(s.astype(BF16), v) + di * _dot(q, c_prev.astype(BF16))
        n_prev = n_scr[h:h + 1, :]
        qn_state = jnp.sum(q.astype(F32) * n_prev.astype(BF16).astype(F32), axis=-1, keepdims=True)
        qn = jnp.sum(s, axis=-1, keepdims=True) + di * qn_state
        hh = num / jnp.maximum(jnp.abs(qn), inv_floor[:, h:h + 1])
        hh = hh * lax.rsqrt(jnp.mean(hh * hh, axis=-1, keepdims=True) + EPS)
        hh = hh * gh_ref[:, h * dv:(h + 1) * dv]
        og = og_ref[0, :, h * dv:(h + 1) * dv]
        o_ref[0, :, h * dv:(h + 1) * dv] = (hh * jax.nn.sigmoid(og)).astype(BF16)

        kw = k.astype(F32) * w_all[:, h:h + 1]
        dec = decay[:, h:h + 1]
        c_scr[h] = dec * c_prev + _dot_tn(kw.astype(BF16), v)
        n_scr[h:h + 1, :] = dec * n_prev + jnp.sum(kw, axis=0, keepdims=True)

    m_scr[...] = m_end

    @pl.when(c == nc - 1)
    def _():
        c_out_ref[0] = c_scr[...]
        n_out_ref[0] = n_scr[...]
        m_out_ref[0] = m_scr[...]


def _mlstm(qk, v, og, gates, bias_row, g_head, c0, n0, m0, valid_len):
    nb, L, vt = v.shape
    dv = vt // MH
    dqk = qk.shape[2] // (2 * MH)
    nc = L // MLSTM_CHUNK
    blk = lambda width: pl.BlockSpec((1, MLSTM_CHUNK, width), lambda b, c: (b, c, 0))
    return pl.pallas_call(
        functools.partial(_mlstm_kernel, valid_len=valid_len, nc=nc, dqk=dqk, dv=dv),
        grid=(nb, nc),
        in_specs=[
            blk(qk.shape[2]), blk(vt), blk(vt), blk(2 * LANES),
            pl.BlockSpec((1, 2 * LANES), lambda b, c: (0, 0)),
            pl.BlockSpec((1, vt), lambda b, c: (0, 0)),
            pl.BlockSpec((1, MH, dqk, dv), lambda b, c: (b, 0, 0, 0)),
            pl.BlockSpec((1, MH, dqk), lambda b, c: (b, 0, 0)),
            pl.BlockSpec((1, 1, LANES), lambda b, c: (b, 0, 0)),
        ],
        out_specs=[
            blk(vt),
            pl.BlockSpec((1, MH, dqk, dv), lambda b, c: (b, 0, 0, 0)),
            pl.BlockSpec((1, MH, dqk), lambda b, c: (b, 0, 0)),
            pl.BlockSpec((1, 1, LANES), lambda b, c: (b, 0, 0)),
        ],
        out_shape=[
            jax.ShapeDtypeStruct((nb, L, vt), BF16),
            jax.ShapeDtypeStruct((nb, MH, dqk, dv), F32),
            jax.ShapeDtypeStruct((nb, MH, dqk), F32),
            jax.ShapeDtypeStruct((nb, 1, LANES), F32),
        ],
        scratch_shapes=[
            pltpu.VMEM((MH, dqk, dv), F32),
            pltpu.VMEM((MH, dqk), F32),
            pltpu.VMEM((1, LANES), F32),
        ],
        compiler_params=_params(("parallel", "arbitrary")),
        name="mlstm_chunks",
    )(qk, v, og, gates, bias_row, g_head.reshape(1, vt), c0, n0, m0)


def _suffix_matrix():
    j = lax.broadcasted_iota(jnp.int32, (2 * SB_BLOCK, 2 * SB_BLOCK), 0) % SB_BLOCK
    s = lax.broadcasted_iota(jnp.int32, (2 * SB_BLOCK, 2 * SB_BLOCK), 1)
    return jnp.where((s >= SB_BLOCK) | (j > s), 1.0, 0.0).astype(BF16)


def _sb_scores(z, mask, suffix_mat):
    soft = jnp.log2(1.0 + jnp.exp2(-jnp.abs(z)))
    log_beta = jnp.minimum(z, 0.0) - soft
    log_1mb = log_beta - z
    if mask is not None:
        log_1mb = jnp.where(mask, log_1mb, 0.0)
    hi = log_1mb.astype(BF16)
    lo = (log_1mb - hi.astype(F32)).astype(BF16)
    sums = _dot(jnp.concatenate([hi, lo], axis=1), suffix_mat)
    return log_beta, sums[:, :SB_BLOCK], sums[:, SB_BLOCK:]


def _sb_visit(zs, masks, value_fns, carry, suffix_mat):
    scores = [_sb_scores(z, m, suffix_mat) for z, m in zip(zs, masks)]
    pv = None
    for (log_beta, later, total), mask, value_fn in zip(scores, masks, value_fns):
        a = jnp.exp2(log_beta + later + carry)
        if mask is not None:
            a = jnp.where(mask, a, 0.0)
        out = value_fn(a.astype(BF16))
        pv = out if pv is None else pv + out
        carry = carry + total
    return pv, carry


def _sb_prompt_kernel(q_ref, k_ref, v_ref, bias_ref, o_ref, k_scr, v_scr, acc_scr, carry_scr,
                      *, n_qt, scale):
    k_scr[...] = k_ref[0].astype(BF16)
    v_scr[...] = v_ref[0].astype(BF16)
    suffix_mat = _suffix_matrix()
    bias = bias_ref[...] * LOG2E
    sub = SB_QTILE // SB_BLOCK

    def visit(r0, r1, blocks, mask):
        q = q_ref[0, r0:r1, :]
        zs, value_fns = [], []
        for j in blocks:
            keys = pl.ds(pl.multiple_of(j * SB_BLOCK, SB_BLOCK), SB_BLOCK)
            zs.append(_dot_nt(q, k_scr[keys, :]) * (scale * LOG2E) + bias)
            value_fns.append(lambda a, keys=keys: _dot(a, v_scr[keys, :]))
        pv, carry = _sb_visit(zs, [mask] * len(zs), value_fns, carry_scr[r0:r1, :], suffix_mat)
        acc_scr[r0:r1, :] += pv
        carry_scr[r0:r1, :] = carry

    acc_scr[...] = jnp.zeros_like(acc_scr)
    carry_scr[...] = jnp.zeros_like(carry_scr)
    for c in reversed(range(sub)):
        rows_left = SB_QTILE - c * SB_BLOCK
        row = lax.broadcasted_iota(jnp.int32, (rows_left, SB_BLOCK), 0)
        col = lax.broadcasted_iota(jnp.int32, (rows_left, SB_BLOCK), 1)
        for t in range(n_qt):
            visit(t * SB_QTILE + c * SB_BLOCK, (t + 1) * SB_QTILE, [t * sub + c], col < row)

    for t in range(1, n_qt):
        older = list(range(t * sub - 1, -1, -1))
        for i in range(0, len(older), SB_OLDER_GROUP):
            visit(t * SB_QTILE, (t + 1) * SB_QTILE, older[i:i + SB_OLDER_GROUP], None)
    o_ref[0] = acc_scr[...].astype(BF16)


def _sb_prompt(q, k, v, bias_lanes):
    nb, L, hd = q.shape
    dh = hd // SB_HEADS
    spec = pl.BlockSpec((1, L, dh), lambda b, h: (b, 0, h))
    return pl.pallas_call(
        functools.partial(_sb_prompt_kernel, n_qt=L // SB_QTILE, scale=dh ** -0.5),
        grid=(nb, SB_HEADS),
        in_specs=[spec, spec, spec, pl.BlockSpec((1, dh), lambda b, h: (0, h))],
        out_specs=spec,
        out_shape=jax.ShapeDtypeStruct((nb, L, hd), BF16),
        scratch_shapes=[pltpu.VMEM((L, dh), BF16), pltpu.VMEM((L, dh), BF16),
                        pltpu.VMEM((L, dh), F32), pltpu.VMEM((L, SB_BLOCK), F32)],
        compiler_params=_params(("parallel", "parallel")),
        name="stickbreak_prompt",
    )(q, k, v, bias_lanes)


def _sb_decode_kernel(pt_ref, q_ref, bias_ref, kn_ref, vn_ref, *rest, n_steps, n_new, scale):
    page_refs = rest[:2 * DECODE_PAGES]
    o_ref, carry_scr, acc_scr, qbd_scr, expand_scr, hmask_scr = rest[2 * DECODE_PAGES:]
    p = pl.program_id(1)
    rows = SB_HEADS * n_new
    dh = q_ref.shape[2] // SB_HEADS
    suffix_mat = _suffix_matrix()

    @pl.when(p == 0)
    def _():
        carry_scr[...] = jnp.zeros_like(carry_scr)
        acc_scr[...] = jnp.zeros_like(acc_scr)
        q_rep = jnp.concatenate([q_ref[0]] * SB_HEADS, axis=0)
        lane_head = lax.broadcasted_iota(jnp.int32, q_rep.shape, 1) // dh
        row_head = lax.broadcasted_iota(jnp.int32, q_rep.shape, 0) // n_new
        qbd_scr[...] = jnp.where(lane_head == row_head, q_rep, 0.0).astype(BF16)
        tok = lax.broadcasted_iota(jnp.int32, expand_scr.shape, 0)
        c = lax.broadcasted_iota(jnp.int32, expand_scr.shape, 1)
        expand_scr[...] = jnp.where(c // SB_HEADS == tok, 1.0, 0.0).astype(BF16)
        r = lax.broadcasted_iota(jnp.int32, hmask_scr.shape, 0)
        c = lax.broadcasted_iota(jnp.int32, hmask_scr.shape, 1)
        hmask_scr[...] = jnp.where(c % SB_HEADS == r // n_new, 1.0, 0.0).astype(BF16)

    def attend(kv_refs, mask):
        zs, value_fns = [], []
        for k_ref, v_ref in kv_refs:
            k_all = jnp.concatenate(
                [k_ref[pl.ds(h, PAGE_SIZE, stride=SB_HEADS), :].astype(BF16) for h in range(SB_HEADS)],
                axis=1)
            zs.append(_dot_nt(qbd_scr[...], k_all) * (scale * LOG2E) + bias_ref[...] * LOG2E)

            def values(a, v_ref=v_ref):
                spread = _dot(a, expand_scr[...]).astype(BF16) * hmask_scr[...]
                return _dot(spread, v_ref[...].astype(BF16))

            value_fns.append(values)
        pv, carry = _sb_visit(zs, [mask] * len(zs), value_fns, carry_scr[...], suffix_mat)
        acc_scr[...] += pv
        carry_scr[...] = carry

    @pl.when(p == 0)
    def _():
        row = lax.broadcasted_iota(jnp.int32, (rows, SB_BLOCK), 0)
        col = lax.broadcasted_iota(jnp.int32, (rows, SB_BLOCK), 1)
        attend([(kn_ref, vn_ref)], col < row % n_new)

    @pl.when(p > 0)
    def _():
        attend([(page_refs[2 * i], page_refs[2 * i + 1]) for i in range(DECODE_PAGES)], None)

    @pl.when(p == n_steps - 1)
    def _():
        o_ref[0] = acc_scr[...]


def _sb_decode(page_table, q, bias_rows, k_new, v_new, pool_k, pool_v, layer):
    nb, n_new, hd = q.shape
    dh = hd // SB_HEADS
    n_pages = page_table.shape[1]
    rows = SB_HEADS * n_new
    prow = PAGE_SIZE * SB_HEADS
    assert n_pages % DECODE_PAGES == 0 and rows == SB_BLOCK and PAGE_SIZE == SB_BLOCK
    n_steps = n_pages // DECODE_PAGES + 1
    n_layers, n_pool = pool_v.shape[:2]
    pool_k = pool_k.reshape(n_layers, n_pool, prow, dh)
    pool_v = pool_v.reshape(n_layers, n_pool, prow, dh)
    k_new = k_new.reshape(nb, prow, dh)
    v_new = v_new.reshape(nb, prow, dh)

    def page_spec(slot):
        return pl.BlockSpec(
            (None, None, prow, dh),
            lambda b, p, pt: (layer, pt[b, n_pages - DECODE_PAGES * jnp.maximum(p, 1) + DECODE_PAGES - 1 - slot], 0, 0))

    new_spec = pl.BlockSpec((None, prow, dh), lambda b, p, pt: (b, 0, 0))
    grid_spec = pltpu.PrefetchScalarGridSpec(
        num_scalar_prefetch=1,
        grid=(nb, n_steps),
        in_specs=[
            pl.BlockSpec((1, n_new, hd), lambda b, p, pt: (b, 0, 0)),
            pl.BlockSpec((rows, SB_BLOCK), lambda b, p, pt: (0, 0)),
            new_spec, new_spec,
        ] + [page_spec(slot) for slot in range(DECODE_PAGES) for _ in range(2)],
        out_specs=pl.BlockSpec((1, rows, dh), lambda b, p, pt: (b, 0, 0)),
        scratch_shapes=[pltpu.VMEM((rows, SB_BLOCK), F32), pltpu.VMEM((rows, dh), F32),
                        pltpu.VMEM((rows, hd), BF16), pltpu.VMEM((PAGE_SIZE, prow), BF16),
                        pltpu.VMEM((rows, prow), BF16)],
    )
    return pl.pallas_call(
        functools.partial(_sb_decode_kernel, n_steps=n_steps, n_new=n_new, scale=dh ** -0.5),
        grid_spec=grid_spec,
        out_shape=jax.ShapeDtypeStruct((nb, rows, dh), F32),
        compiler_params=_params(("parallel", "arbitrary")),
        name="stickbreak_decode",
    )(page_table, q, bias_rows, k_new, v_new, *([pool_k, pool_v] * DECODE_PAGES))


def _ffn(x, mods, g_pre, g_post, w_in, layer_in, w_out, layer_out, tf=512, emit_copy=False):
    d_ff = w_out.shape[1]
    (act,) = _modmm(x, g_pre, mods[3], mods[4], w_in, layer_in, (0, d_ff), tf, d_ff // tf,
                    (BF16,), _swiglu_epilogue)
    return _mm_norm_res(act, w_out, layer_out, x, mods[5], g_post, emit_copy=emit_copy)


def _mlstm_gate_operands(w_in, b_gate):
    D = w_in.shape[0]
    first = w_in.shape[1] - 2 * MH
    w_gate = jnp.zeros((D, 2 * LANES), F32)
    w_gate = w_gate.at[:, :MH].set(w_in[:, first:first + MH])
    w_gate = w_gate.at[:, LANES:LANES + MH].set(w_in[:, first + MH:])
    bias_row = jnp.zeros((1, 2 * LANES), F32)
    bias_row = bias_row.at[0, :MH].set(b_gate[0].astype(F32))
    bias_row = bias_row.at[0, LANES:LANES + MH].set(b_gate[1].astype(F32))
    return w_gate, bias_row


def _mlstm_layer(x, mods, g_norm, w_in, w_out, layer, w_gate, bias_row, g_head, c0, n0, m0,
                 chunk_rows, emit_copy=False):
    dqk = c0.shape[2]
    qk_tot = 2 * MH * dqk
    tn = _proj_tile(x.shape[1], qk_tot, w_in.dtype.itemsize)
    epilogue = [functools.partial(_scaled_q_then_k, q_cols=MH * dqk, q_scale=dqk ** -0.5),
                _as_bf16, _as_f32]
    proj = _modmm(x, g_norm[0], mods[0], mods[1], w_in, layer, (0, qk_tot, 2 * qk_tot), tn,
                  qk_tot // tn, (BF16, BF16, F32), epilogue, w_gate=w_gate, emit_copy=emit_copy)
    qk, v, og, gates = proj[:4]
    m0p = jnp.zeros((m0.shape[0], 1, LANES), F32).at[:, 0, :MH].set(m0.astype(F32))
    if chunk_rows == MLSTM_CHUNK:
        h, c_new, n_new, m_new = _mlstm(qk, v, og, gates, bias_row, g_head,
                                        c0.astype(F32), n0.astype(F32), m0p, MLSTM_CHUNK)
    else:
        nb = x.shape[1] // chunk_rows

        def pad(a):
            a = a.reshape(nb, chunk_rows, a.shape[2])
            return jnp.pad(a, ((0, 0), (0, MLSTM_CHUNK - chunk_rows), (0, 0)))

        h, c_new, n_new, m_new = _mlstm(pad(qk), pad(v), pad(og), pad(gates), bias_row, g_head,
                                        c0.astype(F32), n0.astype(F32), m0p, chunk_rows)
        h = h[:, :chunk_rows].reshape(1, nb * chunk_rows, h.shape[2])
    res = _mm_norm_res(h, w_out, layer, x, mods[2], g_norm[1], emit_copy=emit_copy)
    if emit_copy:
        return res[0], c_new, n_new, m_new[:, 0, :MH], (proj[4], res[1])
    return res, c_new, n_new, m_new[:, 0, :MH], None


def kernel(x_prompt, x_sample, state_C, state_n, state_m, cache_k, cache_v, page_table, c_prompt, c_sample, w_ada, b_ada, g_norm, w_in_a, b_gate_a, g_head_a, w_out_a, w_in_b, b_sb, w_out_b, w_ffn_in, w_ffn_out):
    bp, seq, D = x_prompt.shape
    bs, dec_seq, _ = x_sample.shape
    depth = w_ada.shape[0]
    dh = D // SB_HEADS

    rows = -(-(bp + bs) // 8) * 8
    c_all = jnp.zeros((rows, D), F32).at[:bp].set(c_prompt).at[bp:bp + bs].set(c_sample)
    ada = _ada(c_all, w_ada, b_ada)

    xp = x_prompt
    xs = x_sample.reshape(1, bs * dec_seq, D)
    outs = {name: [] for name in ("kp", "vp", "ks", "vs", "Cp", "np", "mp", "Cs", "ns", "ms")}
    for i in range(depth):
        j = i // 2
        mods_p = [ada[i, :bp, s * D:(s + 1) * D].reshape(bp, 1, D) for s in range(6)]
        mods_s = [jnp.repeat(ada[i, bp:bp + bs, s * D:(s + 1) * D], dec_seq, axis=0).reshape(1, bs * dec_seq, D)
                  for s in range(6)]
        if i % 2 == 0:
            w_gate, bias_row = _mlstm_gate_operands(w_in_a[j], b_gate_a[j])
            xs, Cs, n_s, m_s, (w_in_c, w_out_c) = _mlstm_layer(
                xs, mods_s, g_norm[i], w_in_a, w_out_a, j, w_gate, bias_row, g_head_a[j],
                state_C[j], state_n[j], state_m[j], dec_seq, emit_copy=True)
            zc = jnp.zeros((bp,) + state_C.shape[2:], F32)
            zn = jnp.zeros((bp,) + state_n.shape[2:], F32)
            zm = jnp.zeros((bp,) + state_m.shape[2:], F32)
            xp, Cp, n_p, m_p, _ = _mlstm_layer(
                xp, mods_p, g_norm[i], w_in_c, w_out_c, 0, w_gate, bias_row, g_head_a[j],
                zc, zn, zm, MLSTM_CHUNK)
            outs["Cp"].append(Cp); outs["np"].append(n_p); outs["mp"].append(m_p)
            outs["Cs"].append(Cs); outs["ns"].append(n_s); outs["ms"].append(m_s)
        else:
            bias = b_sb[j].astype(F32)
            tn = _proj_tile(xs.shape[1], D, w_in_b.dtype.itemsize)
            qs, kn, vn, w_in_c = _modmm(xs, g_norm[i, 0], mods_s[0], mods_s[1], w_in_b, j, (0, D, 2 * D), tn,
                                        D // tn, (F32, F32, F32), [_as_f32, _as_f32, _as_f32], emit_copy=True)

            def as_page(a):
                a = a.reshape(bs, dec_seq, SB_HEADS, dh)
                return jnp.pad(a, ((0, 0), (0, PAGE_SIZE - dec_seq), (0, 0), (0, 0)))

            os_ = _sb_decode(page_table, qs.reshape(bs, dec_seq, D),
                             jnp.broadcast_to(jnp.repeat(bias, dec_seq)[:, None], (SB_HEADS * dec_seq, SB_BLOCK)),
                             as_page(kn), as_page(vn), cache_k, cache_v, j)
            os_ = os_.reshape(bs, SB_HEADS, dec_seq, dh).transpose(0, 2, 1, 3).reshape(1, bs * dec_seq, D)
            xs, w_out_c = _mm_norm_res(os_.astype(BF16), w_out_b, j, xs, mods_s[2], g_norm[i, 1], emit_copy=True)
            tn = _proj_tile(seq, D, w_in_c.dtype.itemsize)
            qp, kp, vp = _modmm(xp, g_norm[i, 0], mods_p[0], mods_p[1], w_in_c, 0, (0, D, 2 * D), tn,
                                D // tn, (BF16, F32, F32), [_as_bf16, _as_f32, _as_f32])
            op = _sb_prompt(qp, kp, vp, jnp.repeat(bias, dh).reshape(1, D))
            xp = _mm_norm_res(op, w_out_c, 0, xp, mods_p[2], g_norm[i, 1])
            outs["kp"].append(kp.reshape(bp, seq, SB_HEADS, dh)); outs["vp"].append(vp.reshape(bp, seq, SB_HEADS, dh))
            outs["ks"].append(kn.reshape(bs, dec_seq, SB_HEADS, dh)); outs["vs"].append(vn.reshape(bs, dec_seq, SB_HEADS, dh))
        xs, w_fo_c = _ffn(xs, mods_s, g_norm[i, 2], g_norm[i, 3], w_ffn_in, i, w_ffn_out, i, emit_copy=True)
        xp = _ffn(xp, mods_p, g_norm[i, 2], g_norm[i, 3], w_ffn_in, i, w_fo_c, 0)

    st = lambda name: jnp.stack(outs[name])
    return (xp, xs.reshape(bs, dec_seq, D), st("kp"), st("vp"), st("ks"), st("vs"),
            st("Cp"), st("np"), st("mp"), st("Cs"), st("ns"), st("ms"))
```

```python
import functools

import jax
import jax.numpy as jnp
from jax import lax
from jax.experimental import pallas as pl
from jax.experimental.pallas import tpu as pltpu

F32 = jnp.float32
BF16 = jnp.bfloat16

EPS = 1e-6
MH = 8
MLSTM_CHUNK = 128
SB_HEADS = 16
SB_BLOCK = 128
SB_QTILE = 512
SB_OLDER_GROUP = 8
SB_OWN_GROUP = 2
PAGE_SIZE = 128
NEG_BIG = -1e30
LOG2E = 1.4426950408889634

LANES = 128
SUBLANES = 8
VMEM_LIMIT = 52 * 1024 * 1024
ACC_COLS = 512
PROJ_TILE = 512
DECODE_PAGES = 8
W_BLOCK_BYTES = 6 * 1024 * 1024


def _dot(a, b):
    return jnp.dot(a, b, preferred_element_type=F32)


def _dot_nt(a, b):
    return lax.dot_general(a, b, (((1,), (1,)), ((), ())), preferred_element_type=F32)


def _dot_tn(a, b):
    return lax.dot_general(a, b, (((0,), (0,)), ((), ())), preferred_element_type=F32)


def _params(sem):
    return pltpu.CompilerParams(dimension_semantics=sem, vmem_limit_bytes=VMEM_LIMIT)


def _ada_kernel(c_ref, w_ref, b_ref, o_ref):
    c = c_ref[...]
    a = (c * jax.nn.sigmoid(c)).astype(BF16)
    o_ref[0] = _dot(a, w_ref[0].astype(BF16)) + b_ref[0]


def _ada(c_all, w_ada, b_ada, tn=1024):
    depth, d, n = w_ada.shape
    rows = c_all.shape[0]
    return pl.pallas_call(
        _ada_kernel,
        grid=(depth, n // tn),
        in_specs=[
            pl.BlockSpec((rows, d), lambda l, j: (0, 0)),
            pl.BlockSpec((1, d, tn), lambda l, j: (l, 0, j)),
            pl.BlockSpec((1, 1, tn), lambda l, j: (l, 0, j)),
        ],
        out_specs=pl.BlockSpec((1, rows, tn), lambda l, j: (l, 0, j)),
        out_shape=jax.ShapeDtypeStruct((depth, rows, n), F32),
        compiler_params=_params(("parallel", "parallel")),
        name="ada_params",
    )(c_all, w_ada, b_ada.reshape(depth, 1, n))


def _row_chunks(tl):
    rc = min(tl, 256)
    return rc, tl // rc


def _for_row_chunks(tl, body):
    rc, n_chunks = _row_chunks(tl)
    if n_chunks == 1:
        body(pl.ds(0, rc))
    else:
        pl.loop(0, n_chunks)(lambda c: body(pl.ds(pl.multiple_of(c * rc, rc), rc)))


def _store_inv_rms(read_rows, r_scr, tl):
    def body(rows):
        v = read_rows(rows)
        r = lax.rsqrt(jnp.mean(v * v, axis=-1, keepdims=True) + EPS)
        r_scr[rows, :] = jnp.broadcast_to(r, (r.shape[0], LANES))

    _for_row_chunks(tl, body)


def _modmm_kernel(*refs, n_slices, n_tiles, epilogue, sequential, has_gate, emit_copy, tl, per_row):
    x_ref, g_ref, sh_ref, sc_ref = refs[:4]
    n_w = 1 if sequential else n_slices
    w_refs = refs[4:4 + n_w]
    pos = 4 + n_w
    if has_gate:
        wg_ref = refs[pos]
        pos += 1
    n_out = len(refs) - pos - 2 - (1 if has_gate else 0) - (1 if emit_copy else 0)
    out_refs = refs[pos:pos + n_out]
    pos += n_out
    if has_gate:
        go_ref = refs[pos]
        pos += 1
    if emit_copy:
        wcopy_ref = refs[pos]
        pos += 1
    h_scr, r_scr = refs[pos], refs[pos + 1]
    j = pl.program_id(2)

    @pl.when(j == 0)
    def _():
        _store_inv_rms(lambda rows: x_ref[0, rows, :], r_scr, tl)

        def modulate(rows):
            if per_row:
                sc, sh = sc_ref[0, rows, :], sh_ref[0, rows, :]
            else:
                sc, sh = sc_ref[0], sh_ref[0]
            gain = g_ref[0] * (1.0 + sc)
            h = (x_ref[0, rows, :] * r_scr[rows, :1]) * gain + sh
            h_scr[rows, :] = h.astype(BF16)

        _for_row_chunks(tl, modulate)
        if has_gate:
            go_ref[0] = _dot(h_scr[...], wg_ref[...].astype(BF16))

    if sequential:
        if emit_copy:
            wcopy_ref[...] = w_refs[0][...].astype(BF16)
            rhs_ref = wcopy_ref
        else:
            rhs_ref = w_refs[0]
        for s, out_ref in enumerate(out_refs):
            @pl.when(j // n_tiles == s)
            def _(s=s, out_ref=out_ref):
                y = _dot(h_scr[...], rhs_ref[...].astype(BF16))
                out_ref[0] = epilogue[s](y, j - s * n_tiles)
    else:
        hb = h_scr[...]
        ys = [_dot(hb, w_ref[...].astype(BF16)) for w_ref in w_refs]
        epilogue(ys, out_refs, j)


def _modmm(x, g, shift, scale, w, layer, slice_starts, tn, n_tiles, out_dtypes, epilogue,
           w_gate=None, tl=1024, emit_copy=False):
    nb, L, D = x.shape
    tl = min(tl, L)
    per_row = shift.shape[1] != 1
    r = tl if per_row else 1
    n_slices = len(slice_starts)
    has_gate = w_gate is not None
    sequential = isinstance(epilogue, (list, tuple))
    if sequential:
        assert all(s0 == slice_starts[0] + s * n_tiles * tn for s, s0 in enumerate(slice_starts))

    def mod_map(b, i, j):
        return (b, i if per_row else 0, 0)

    in_specs = [
        pl.BlockSpec((1, tl, D), lambda b, i, j: (b, i, 0)),
        pl.BlockSpec((1, 1, D), lambda b, i, j: (0, 0, 0)),
        pl.BlockSpec((1, r, D), mod_map),
        pl.BlockSpec((1, r, D), mod_map),
    ]
    args = [x, g.reshape(1, 1, D), shift, scale]
    for s0 in slice_starts[:1] if sequential else slice_starts:
        off = s0 // tn
        in_specs.append(pl.BlockSpec((None, D, tn), lambda b, i, j, off=off: (layer, 0, off + j)))
        args.append(w)
    if has_gate:
        gw = w_gate.shape[1]
        in_specs.append(pl.BlockSpec((D, gw), lambda b, i, j: (0, 0)))
        args.append(w_gate)
    if sequential:
        out_specs = [
            pl.BlockSpec((1, tl, tn), lambda b, i, j, s=s: (b, i, jnp.clip(j - s * n_tiles, 0, n_tiles - 1)))
            for s in range(n_slices)]
    else:
        out_specs = [pl.BlockSpec((1, tl, tn), lambda b, i, j: (b, i, j)) for _ in out_dtypes]
    out_shape = [jax.ShapeDtypeStruct((nb, L, n_tiles * tn), dt) for dt in out_dtypes]
    if has_gate:
        out_specs.append(pl.BlockSpec((1, tl, gw), lambda b, i, j: (b, i, 0)))
        out_shape.append(jax.ShapeDtypeStruct((nb, L, gw), F32))
    if emit_copy:
        assert sequential and nb == 1 and L == tl and slice_starts[0] == 0
        out_specs.append(pl.BlockSpec((None, D, tn), lambda b, i, j: (0, 0, j)))
        out_shape.append(jax.ShapeDtypeStruct((1, D, n_slices * n_tiles * tn), BF16))
    return pl.pallas_call(
        functools.partial(_modmm_kernel, n_slices=n_slices, n_tiles=n_tiles, epilogue=epilogue,
                          sequential=sequential, has_gate=has_gate, emit_copy=emit_copy, tl=tl,
                          per_row=per_row),
        grid=(nb, L // tl, n_tiles * n_slices if sequential else n_tiles),
        in_specs=in_specs,
        out_specs=out_specs,
        out_shape=out_shape,
        scratch_shapes=[pltpu.VMEM((tl, D), BF16), pltpu.VMEM((tl, LANES), F32)],
        compiler_params=_params(("parallel", "parallel", "arbitrary")),
        name="modulated_matmul",
    )(*args)


def _as_bf16(y, tile):
    return y.astype(BF16)


def _as_f32(y, tile):
    return y


def _scaled_q_then_k(y, tile, *, q_cols, q_scale):
    tn = y.shape[1]
    if q_cols % tn == 0:
        factor = jnp.where(tile < q_cols // tn, q_scale, 1.0).astype(F32)
    else:
        col = tile * tn + lax.broadcasted_iota(jnp.int32, (1, tn), 1)
        factor = jnp.where(col < q_cols, q_scale, 1.0).astype(F32)
    return (y * factor).astype(BF16)


def _proj_tile(rows, width, itemsize):
    if rows >= 2 * PROJ_TILE:
        return PROJ_TILE
    return width if itemsize <= 2 else width // 2


def _swiglu_epilogue(ys, out_refs, j):
    gate, up = ys
    out_refs[0][0] = ((gate * jax.nn.sigmoid(gate)) * up).astype(BF16)


def _mm_norm_res_kernel(a_ref, w_ref, x_ref, gate_ref, g_ref, o_ref, *rest, tl, nk, per_row, emit_copy):
    r_scr = rest[-1]
    k = pl.program_id(2)

    @pl.when(k == 0)
    def _():
        o_ref[...] = jnp.zeros_like(o_ref)

    if emit_copy:
        rest[0][...] = w_ref[...].astype(BF16)
        w_ref = rest[0]
    a = a_ref[0]
    cw = min(ACC_COLS, o_ref.shape[2])
    for c in range(o_ref.shape[2] // cw):
        cols = slice(c * cw, (c + 1) * cw)
        o_ref[0, :, cols] += _dot(a, w_ref[:, cols].astype(BF16))

    @pl.when(k == nk - 1)
    def _():
        _store_inv_rms(lambda rows: o_ref[0, rows, :], r_scr, tl)

        def finish(rows):
            gate = gate_ref[0, rows, :] if per_row else gate_ref[0]
            gain = gate * g_ref[0]
            o_ref[0, rows, :] = x_ref[0, rows, :] + (o_ref[0, rows, :] * r_scr[rows, :1]) * gain

        _for_row_chunks(tl, finish)


def _k_tile(K, D, itemsize):
    limit = max(W_BLOCK_BYTES // (D * itemsize), LANES)
    return max(t for t in range(LANES, K + 1, LANES) if K % t == 0 and t <= limit)


def _mm_norm_res(a, w, layer, x, gate, g, tl=1024, emit_copy=False):
    nb, L, K = a.shape
    D = w.shape[2]
    tl = min(tl, L)
    per_row = gate.shape[1] != 1
    r = tl if per_row else 1
    tk = _k_tile(K, D, w.dtype.itemsize)
    nk = K // tk
    out_specs = [pl.BlockSpec((1, tl, D), lambda b, i, k: (b, i, 0))]
    out_shape = [jax.ShapeDtypeStruct((nb, L, D), F32)]
    if emit_copy:
        assert nb == 1 and L == tl
        out_specs.append(pl.BlockSpec((None, tk, D), lambda b, i, k: (0, k, 0)))
        out_shape.append(jax.ShapeDtypeStruct((1, K, D), BF16))
    outs = pl.pallas_call(
        functools.partial(_mm_norm_res_kernel, tl=tl, nk=nk, per_row=per_row, emit_copy=emit_copy),
        grid=(nb, L // tl, nk),
        in_specs=[
            pl.BlockSpec((1, tl, tk), lambda b, i, k: (b, i, k)),
            pl.BlockSpec((None, tk, D), lambda b, i, k: (layer, k, 0)),
            pl.BlockSpec((1, tl, D), lambda b, i, k: (b, i, 0)),
            pl.BlockSpec((1, r, D), lambda b, i, k: (b, i if per_row else 0, 0)),
            pl.BlockSpec((1, 1, D), lambda b, i, k: (0, 0, 0)),
        ],
        out_specs=out_specs,
        out_shape=out_shape,
        scratch_shapes=[pltpu.VMEM((tl, LANES), F32)],
        compiler_params=_params(("parallel", "parallel", "arbitrary")),
        name="matmul_norm_residual",
    )(a, w, x, gate, g.reshape(1, 1, D))
    return tuple(outs) if emit_copy else outs[0]


def _scan_rows(x, op, row):
    d = 1
    while d < x.shape[0]:
        shifted = pltpu.roll(x, d, axis=0)
        x = jnp.where(row >= d, op(x, shifted), x)
        d *= 2
    return x


def _mlstm_kernel(qk_ref, v_ref, og_ref, gt_ref, bias_ref, gh_ref, c0_ref, n0_ref, m0_ref,
                  o_ref, c_out_ref, n_out_ref, m_out_ref, c_scr, n_scr, m_scr,
                  *, valid_len, nc, dqk, dv):
    c = pl.program_id(1)
    L = MLSTM_CHUNK

    @pl.when(c == 0)
    def _():
        c_scr[...] = c0_ref[0]
        n_scr[...] = n0_ref[0]
        m_scr[...] = m0_ref[0]

    gates = gt_ref[0] + bias_ref[...]
    li = gates[:, :LANES]
    gf = gates[:, LANES:]
    lf = jnp.minimum(gf, 0.0) - jnp.log1p(jnp.exp(-jnp.abs(gf)))
    row = lax.broadcasted_iota(jnp.int32, (L, LANES), 0)
    col = lax.broadcasted_iota(jnp.int32, (L, LANES), 1)
    if valid_len < L:
        li = jnp.where(row < valid_len, li, NEG_BIG)
        lf = jnp.where(row < valid_len, lf, 0.0)

    b = _scan_rows(lf, jnp.add, row)
    u = li - b
    m_prev = m_scr[...]
    m_t = b + jnp.maximum(m_prev, _scan_rows(u, jnp.maximum, row))
    d_inter = jnp.exp(b + m_prev - m_t)
    ct = b - m_t
    inv_floor = jnp.exp(-m_t)
    b_end = b[L - 1:L, :]
    m_end = m_t[L - 1:L, :]
    w_all = jnp.exp(b_end - b + li - m_end)
    decay = jnp.exp(b_end + m_prev - m_end)
    u_t = u.T
    causal = row >= col

    for h in range(MH):
        q = qk_ref[0, :, h * dqk:(h + 1) * dqk]
        k = qk_ref[0, :, MH * dqk + h * dqk:MH * dqk + (h + 1) * dqk]
        v = v_ref[0, :, h * dv:(h + 1) * dv]
        log_d = jnp.where(causal, ct[:, h:h + 1] + u_t[h:h + 1, :], NEG_BIG)
        s = _dot_nt(q, k) * jnp.exp(log_d)
        di = d_inter[:, h:h + 1]
        c_prev = c_scr[h]
        num = _dot(s.astype(BF16), v) + di * _dot(q, c_prev.astype(BF16))
        n_prev = n_scr[h:h + 1, :]
        qn_state = jnp.sum(q.astype(F32) * n_prev.astype(BF16).astype(F32), axis=-1, keepdims=True)
        qn = jnp.sum(s, axis=-1, keepdims=True) + di * qn_state
        hh = num / jnp.maximum(jnp.abs(qn), inv_floor[:, h:h + 1])
        hh = hh * lax.rsqrt(jnp.mean(hh * hh, axis=-1, keepdims=True) + EPS)
        hh = hh * gh_ref[:, h * dv:(h + 1) * dv]
        og = og_ref[0, :, h * dv:(h + 1) * dv]
        o_ref[0, :, h * dv:(h + 1) * dv] = (hh * jax.nn.sigmoid(og)).astype(BF16)

        kw = k.astype(F32) * w_all[:, h:h + 1]
        dec = decay[:, h:h + 1]
        c_scr[h] = dec * c_prev + _dot_tn(kw.astype(BF16), v)
        n_scr[h:h + 1, :] = dec * n_prev + jnp.sum(kw, axis=0, keepdims=True)

    m_scr[...] = m_end

    @pl.when(c == nc - 1)
    def _():
        c_out_ref[0] = c_scr[...]
        n_out_ref[0] = n_scr[...]
        m_out_ref[0] = m_scr[...]


def _mlstm(qk, v, og, gates, bias_row, g_head, c0, n0, m0, valid_len):
    nb, L, vt = v.shape
    dv = vt // MH
    dqk = qk.shape[2] // (2 * MH)
    nc = L // MLSTM_CHUNK
    blk = lambda width: pl.BlockSpec((1, MLSTM_CHUNK, width), lambda b, c: (b, c, 0))
    return pl.pallas_call(
        functools.partial(_mlstm_kernel, valid_len=valid_len, nc=nc, dqk=dqk, dv=dv),
        grid=(nb, nc),
        in_specs=[
            blk(qk.shape[2]), blk(vt), blk(vt), blk(2 * LANES),
            pl.BlockSpec((1, 2 * LANES), lambda b, c: (0, 0)),
            pl.BlockSpec((1, vt), lambda b, c: (0, 0)),
            pl.BlockSpec((1, MH, dqk, dv), lambda b, c: (b, 0, 0, 0)),
            pl.BlockSpec((1, MH, dqk), lambda b, c: (b, 0, 0)),
            pl.BlockSpec((1, 1, LANES), lambda b, c: (b, 0, 0)),
        ],
        out_specs=[
            blk(vt),
            pl.BlockSpec((1, MH, dqk, dv), lambda b, c: (b, 0, 0, 0)),
            pl.BlockSpec((1, MH, dqk), lambda b, c: (b, 0, 0)),
            pl.BlockSpec((1, 1, LANES), lambda b, c: (b, 0, 0)),
        ],
        out_shape=[
            jax.ShapeDtypeStruct((nb, L, vt), BF16),
            jax.ShapeDtypeStruct((nb, MH, dqk, dv), F32),
            jax.ShapeDtypeStruct((nb, MH, dqk), F32),
            jax.ShapeDtypeStruct((nb, 1, LANES), F32),
        ],
        scratch_shapes=[
            pltpu.VMEM((MH, dqk, dv), F32),
            pltpu.VMEM((MH, dqk), F32),
            pltpu.VMEM((1, LANES), F32),
        ],
        compiler_params=_params(("parallel", "arbitrary")),
        name="mlstm_chunks",
    )(qk, v, og, gates, bias_row, g_head.reshape(1, vt), c0, n0, m0)


def _suffix_matrix():
    j = lax.broadcasted_iota(jnp.int32, (2 * SB_BLOCK, 2 * SB_BLOCK), 0) % SB_BLOCK
    s = lax.broadcasted_iota(jnp.int32, (2 * SB_BLOCK, 2 * SB_BLOCK), 1)
    return jnp.where((s >= SB_BLOCK) | (j > s), 1.0, 0.0).astype(BF16)


def _sb_scores(z, mask, suffix_mat):
    soft = jnp.log2(1.0 + jnp.exp2(-jnp.abs(z)))
    log_beta = jnp.minimum(z, 0.0) - soft
    log_1mb = log_beta - z
    if mask is not None:
        log_1mb = jnp.where(mask, log_1mb, 0.0)
    hi = log_1mb.astype(BF16)
    lo = (log_1mb - hi.astype(F32)).astype(BF16)
    sums = _dot(jnp.concatenate([hi, lo], axis=1), suffix_mat)
    return log_beta, sums[:, :SB_BLOCK], sums[:, SB_BLOCK:]


def _sb_visit(zs, masks, value_fns, carry, suffix_mat):
    scores = [_sb_scores(z, m, suffix_mat) for z, m in zip(zs, masks)]
    pv = None
    for (log_beta, later, total), mask, value_fn in zip(scores, masks, value_fns):
        a = jnp.exp2(log_beta + later + carry)
        if mask is not None:
            a = jnp.where(mask, a, 0.0)
        out = value_fn(a.astype(BF16))
        pv = out if pv is None else pv + out
        carry = carry + total
    return pv, carry


def _sb_prompt_kernel(q_ref, k_ref, v_ref, bias_ref, o_ref, k_scr, v_scr, acc_scr, carry_scr,
                      *, n_qt, scale):
    k_scr[...] = k_ref[0].astype(BF16)
    v_scr[...] = v_ref[0].astype(BF16)
    suffix_mat = _suffix_matrix()
    bias = bias_ref[...] * LOG2E
    sub = SB_QTILE // SB_BLOCK

    def visit(r0, r1, blocks, mask):
        q = q_ref[0, r0:r1, :]
        zs, value_fns = [], []
        for j in blocks:
            keys = pl.ds(pl.multiple_of(j * SB_BLOCK, SB_BLOCK), SB_BLOCK)
            zs.append(_dot_nt(q, k_scr[keys, :]) * (scale * LOG2E) + bias)
            value_fns.append(lambda a, keys=keys: _dot(a, v_scr[keys, :]))
        masks = mask if isinstance(mask, list) else [mask] * len(zs)
        pv, carry = _sb_visit(zs, masks, value_fns, carry_scr[r0:r1, :], suffix_mat)
        acc_scr[r0:r1, :] += pv
        carry_scr[r0:r1, :] = carry

    acc_scr[...] = jnp.zeros_like(acc_scr)
    carry_scr[...] = jnp.zeros_like(carry_scr)
    for c_hi in range(sub - 1, -1, -SB_OWN_GROUP):
        group = list(range(c_hi, c_hi - SB_OWN_GROUP, -1))
        first = group[-1] * SB_BLOCK
        rows_left = SB_QTILE - first
        row = lax.broadcasted_iota(jnp.int32, (rows_left, SB_BLOCK), 0) + first
        col = lax.broadcasted_iota(jnp.int32, (rows_left, SB_BLOCK), 1)
        masks = [col + c * SB_BLOCK < row for c in group]
        for t in range(n_qt):
            visit(t * SB_QTILE + first, (t + 1) * SB_QTILE, [t * sub + c for c in group], masks)

    for t in range(1, n_qt):
        older = list(range(t * sub - 1, -1, -1))
        for i in range(0, len(older), SB_OLDER_GROUP):
            visit(t * SB_QTILE, (t + 1) * SB_QTILE, older[i:i + SB_OLDER_GROUP], None)
    o_ref[0] = acc_scr[...].astype(BF16)


def _sb_prompt(q, k, v, bias_lanes):
    nb, L, hd = q.shape
    dh = hd // SB_HEADS
    spec = pl.BlockSpec((1, L, dh), lambda b, h: (b, 0, h))
    return pl.pallas_call(
        functools.partial(_sb_prompt_kernel, n_qt=L // SB_QTILE, scale=dh ** -0.5),
        grid=(nb, SB_HEADS),
        in_specs=[spec, spec, spec, pl.BlockSpec((1, dh), lambda b, h: (0, h))],
        out_specs=spec,
        out_shape=jax.ShapeDtypeStruct((nb, L, hd), BF16),
        scratch_shapes=[pltpu.VMEM((L, dh), BF16), pltpu.VMEM((L, dh), BF16),
                        pltpu.VMEM((L, dh), F32), pltpu.VMEM((L, SB_BLOCK), F32)],
        compiler_params=_params(("parallel", "parallel")),
        name="stickbreak_prompt",
    )(q, k, v, bias_lanes)


def _sb_decode_kernel(pt_ref, q_ref, bias_ref, kn_ref, vn_ref, *rest, n_steps, n_new, scale):
    page_refs = rest[:2 * DECODE_PAGES]
    o_ref, carry_scr, acc_scr, qbd_scr, expand_scr, hmask_scr = rest[2 * DECODE_PAGES:]
    p = pl.program_id(1)
    rows = SB_HEADS * n_new
    dh = q_ref.shape[2] // SB_HEADS
    suffix_mat = _suffix_matrix()

    @pl.when(p == 0)
    def _():
        carry_scr[...] = jnp.zeros_like(carry_scr)
        acc_scr[...] = jnp.zeros_like(acc_scr)
        q_rep = jnp.concatenate([q_ref[0]] * SB_HEADS, axis=0)
        lane_head = lax.broadcasted_iota(jnp.int32, q_rep.shape, 1) // dh
        row_head = lax.broadcasted_iota(jnp.int32, q_rep.shape, 0) // n_new
        qbd_scr[...] = jnp.where(lane_head == row_head, q_rep, 0.0).astype(BF16)
        tok = lax.broadcasted_iota(jnp.int32, expand_scr.shape, 0)
        c = lax.broadcasted_iota(jnp.int32, expand_scr.shape, 1)
        expand_scr[...] = jnp.where(c // SB_HEADS == tok, 1.0, 0.0).astype(BF16)
        r = lax.broadcasted_iota(jnp.int32, hmask_scr.shape, 0)
        c = lax.broadcasted_iota(jnp.int32, hmask_scr.shape, 1)
        hmask_scr[...] = jnp.where(c % SB_HEADS == r // n_new, 1.0, 0.0).astype(BF16)

    def attend(kv_refs, mask):
        zs, value_fns = [], []
        for k_ref, v_ref in kv_refs:
            k_all = jnp.concatenate(
                [k_ref[pl.ds(h, PAGE_SIZE, stride=SB_HEADS), :].astype(BF16) for h in range(SB_HEADS)],
                axis=1)
            zs.append(_dot_nt(qbd_scr[...], k_all) * (scale * LOG2E) + bias_ref[...] * LOG2E)

            def values(a, v_ref=v_ref):
                spread = _dot(a, expand_scr[...]).astype(BF16) * hmask_scr[...]
                return _dot(spread, v_ref[...].astype(BF16))

            value_fns.append(values)
        pv, carry = _sb_visit(zs, [mask] * len(zs), value_fns, carry_scr[...], suffix_mat)
        acc_scr[...] += pv
        carry_scr[...] = carry

    @pl.when(p == 0)
    def _():
        row = lax.broadcasted_iota(jnp.int32, (rows, SB_BLOCK), 0)
        col = lax.broadcasted_iota(jnp.int32, (rows, SB_BLOCK), 1)
        attend([(kn_ref, vn_ref)], col < row % n_new)

    @pl.when(p > 0)
    def _():
        attend([(page_refs[2 * i], page_refs[2 * i + 1]) for i in range(DECODE_PAGES)], None)

    @pl.when(p == n_steps - 1)
    def _():
        o_ref[0] = acc_scr[...]


def _sb_decode(page_table, q, bias_rows, k_new, v_new, pool_k, pool_v, layer):
    nb, n_new, hd = q.shape
    dh = hd // SB_HEADS
    n_pages = page_table.shape[1]
    rows = SB_HEADS * n_new
    prow = PAGE_SIZE * SB_HEADS
    assert n_pages % DECODE_PAGES == 0 and rows == SB_BLOCK and PAGE_SIZE == SB_BLOCK
    n_steps = n_pages // DECODE_PAGES + 1
    n_layers, n_pool = pool_v.shape[:2]
    pool_k = pool_k.reshape(n_layers, n_pool, prow, dh)
    pool_v = pool_v.reshape(n_layers, n_pool, prow, dh)
    k_new = k_new.reshape(nb, prow, dh)
    v_new = v_new.reshape(nb, prow, dh)

    def page_spec(slot):
        return pl.BlockSpec(
            (None, None, prow, dh),
            lambda b, p, pt: (layer, pt[b, n_pages - DECODE_PAGES * jnp.maximum(p, 1) + DECODE_PAGES - 1 - slot], 0, 0))

    new_spec = pl.BlockSpec((None, prow, dh), lambda b, p, pt: (b, 0, 0))
    grid_spec = pltpu.PrefetchScalarGridSpec(
        num_scalar_prefetch=1,
        grid=(nb, n_steps),
        in_specs=[
            pl.BlockSpec((1, n_new, hd), lambda b, p, pt: (b, 0, 0)),
            pl.BlockSpec((rows, SB_BLOCK), lambda b, p, pt: (0, 0)),
            new_spec, new_spec,
        ] + [page_spec(slot) for slot in range(DECODE_PAGES) for _ in range(2)],
        out_specs=pl.BlockSpec((1, rows, dh), lambda b, p, pt: (b, 0, 0)),
        scratch_shapes=[pltpu.VMEM((rows, SB_BLOCK), F32), pltpu.VMEM((rows, dh), F32),
                        pltpu.VMEM((rows, hd), BF16), pltpu.VMEM((PAGE_SIZE, prow), BF16),
                        pltpu.VMEM((rows, prow), BF16)],
    )
    return pl.pallas_call(
        functools.partial(_sb_decode_kernel, n_steps=n_steps, n_new=n_new, scale=dh ** -0.5),
        grid_spec=grid_spec,
        out_shape=jax.ShapeDtypeStruct((nb, rows, dh), F32),
        compiler_params=_params(("parallel", "arbitrary")),
        name="stickbreak_decode",
    )(page_table, q, bias_rows, k_new, v_new, *([pool_k, pool_v] * DECODE_PAGES))


def _ffn(x, mods, g_pre, g_post, w_in, layer_in, w_out, layer_out, tf=512, emit_copy=False):
    d_ff = w_out.shape[1]
    (act,) = _modmm(x, g_pre, mods[3], mods[4], w_in, layer_in, (0, d_ff), tf, d_ff // tf,
                    (BF16,), _swiglu_epilogue)
    return _mm_norm_res(act, w_out, layer_out, x, mods[5], g_post, emit_copy=emit_copy)


def _mlstm_gate_operands(w_in, b_gate):
    D = w_in.shape[0]
    first = w_in.shape[1] - 2 * MH
    w_gate = jnp.zeros((D, 2 * LANES), F32)
    w_gate = w_gate.at[:, :MH].set(w_in[:, first:first + MH])
    w_gate = w_gate.at[:, LANES:LANES + MH].set(w_in[:, first + MH:])
    bias_row = jnp.zeros((1, 2 * LANES), F32)
    bias_row = bias_row.at[0, :MH].set(b_gate[0].astype(F32))
    bias_row = bias_row.at[0, LANES:LANES + MH].set(b_gate[1].astype(F32))
    return w_gate, bias_row


def _mlstm_layer(x, mods, g_norm, w_in, w_out, layer, w_gate, bias_row, g_head, c0, n0, m0,
                 chunk_rows, emit_copy=False):
    dqk = c0.shape[2]
    qk_tot = 2 * MH * dqk
    tn = _proj_tile(x.shape[1], qk_tot, w_in.dtype.itemsize)
    epilogue = [functools.partial(_scaled_q_then_k, q_cols=MH * dqk, q_scale=dqk ** -0.5),
                _as_bf16, _as_f32]
    proj = _modmm(x, g_norm[0], mods[0], mods[1], w_in, layer, (0, qk_tot, 2 * qk_tot), tn,
                  qk_tot // tn, (BF16, BF16, F32), epilogue, w_gate=w_gate, emit_copy=emit_copy)
    qk, v, og, gates = proj[:4]
    m0p = jnp.zeros((m0.shape[0], 1, LANES), F32).at[:, 0, :MH].set(m0.astype(F32))
    if chunk_rows == MLSTM_CHUNK:
        h, c_new, n_new, m_new = _mlstm(qk, v, og, gates, bias_row, g_head,
                                        c0.astype(F32), n0.astype(F32), m0p, MLSTM_CHUNK)
    else:
        nb = x.shape[1] // chunk_rows

        def pad(a):
            a = a.reshape(nb, chunk_rows, a.shape[2])
            return jnp.pad(a, ((0, 0), (0, MLSTM_CHUNK - chunk_rows), (0, 0)))

        h, c_new, n_new, m_new = _mlstm(pad(qk), pad(v), pad(og), pad(gates), bias_row, g_head,
                                        c0.astype(F32), n0.astype(F32), m0p, chunk_rows)
        h = h[:, :chunk_rows].reshape(1, nb * chunk_rows, h.shape[2])
    res = _mm_norm_res(h, w_out, layer, x, mods[2], g_norm[1], emit_copy=emit_copy)
    if emit_copy:
        return res[0], c_new, n_new, m_new[:, 0, :MH], (proj[4], res[1])
    return res, c_new, n_new, m_new[:, 0, :MH], None


def kernel(x_prompt, x_sample, state_C, state_n, state_m, cache_k, cache_v, page_table, c_prompt, c_sample, w_ada, b_ada, g_norm, w_in_a, b_gate_a, g_head_a, w_out_a, w_in_b, b_sb, w_out_b, w_ffn_in, w_ffn_out):
    bp, seq, D = x_prompt.shape
    bs, dec_seq, _ = x_sample.shape
    depth = w_ada.shape[0]
    dh = D // SB_HEADS

    rows = -(-(bp + bs) // 8) * 8
    c_all = jnp.zeros((rows, D), F32).at[:bp].set(c_prompt).at[bp:bp + bs].set(c_sample)
    ada = _ada(c_all, w_ada, b_ada)

    xp = x_prompt
    xs = x_sample.reshape(1, bs * dec_seq, D)
    outs = {name: [] for name in ("kp", "vp", "ks", "vs", "Cp", "np", "mp", "Cs", "ns", "ms")}
    for i in range(depth):
        j = i // 2
        mods_p = [ada[i, :bp, s * D:(s + 1) * D].reshape(bp, 1, D) for s in range(6)]
        mods_s = [jnp.repeat(ada[i, bp:bp + bs, s * D:(s + 1) * D], dec_seq, axis=0).reshape(1, bs * dec_seq, D)
                  for s in range(6)]
        if i % 2 == 0:
            w_gate, bias_row = _mlstm_gate_operands(w_in_a[j], b_gate_a[j])
            xs, Cs, n_s, m_s, (w_in_c, w_out_c) = _mlstm_layer(
                xs, mods_s, g_norm[i], w_in_a, w_out_a, j, w_gate, bias_row, g_head_a[j],
                state_C[j], state_n[j], state_m[j], dec_seq, emit_copy=True)
            zc = jnp.zeros((bp,) + state_C.shape[2:], F32)
            zn = jnp.zeros((bp,) + state_n.shape[2:], F32)
            zm = jnp.zeros((bp,) + state_m.shape[2:], F32)
            xp, Cp, n_p, m_p, _ = _mlstm_layer(
                xp, mods_p, g_norm[i], w_in_c, w_out_c, 0, w_gate, bias_row, g_head_a[j],
                zc, zn, zm, MLSTM_CHUNK)
            outs["Cp"].append(Cp); outs["np"].append(n_p); outs["mp"].append(m_p)
            outs["Cs"].append(Cs); outs["ns"].append(n_s); outs["ms"].append(m_s)
        else:
            bias = b_sb[j].astype(F32)
            tn = _proj_tile(xs.shape[1], D, w_in_b.dtype.itemsize)
            qs, kn, vn, w_in_c = _modmm(xs, g_norm[i, 0], mods_s[0], mods_s[1], w_in_b, j, (0, D, 2 * D), tn,
                                        D // tn, (F32, F32, F32), [_as_f32, _as_f32, _as_f32], emit_copy=True)

            def as_page(a):
                a = a.reshape(bs, dec_seq, SB_HEADS, dh)
                return jnp.pad(a, ((0, 0), (0, PAGE_SIZE - dec_seq), (0, 0), (0, 0)))

            os_ = _sb_decode(page_table, qs.reshape(bs, dec_seq, D),
                             jnp.broadcast_to(jnp.repeat(bias, dec_seq)[:, None], (SB_HEADS * dec_seq, SB_BLOCK)),
                             as_page(kn), as_page(vn), cache_k, cache_v, j)
            os_ = os_.reshape(bs, SB_HEADS, dec_seq, dh).transpose(0, 2, 1, 3).reshape(1, bs * dec_seq, D)
            xs, w_out_c = _mm_norm_res(os_.astype(BF16), w_out_b, j, xs, mods_s[2], g_norm[i, 1], emit_copy=True)
            tn = _proj_tile(seq, D, w_in_c.dtype.itemsize)
            qp, kp, vp = _modmm(xp, g_norm[i, 0], mods_p[0], mods_p[1], w_in_c, 0, (0, D, 2 * D), tn,
                                D // tn, (BF16, F32, F32), [_as_bf16, _as_f32, _as_f32])
            op = _sb_prompt(qp, kp, vp, jnp.repeat(bias, dh).reshape(1, D))
            xp = _mm_norm_res(op, w_out_c, 0, xp, mods_p[2], g_norm[i, 1])
            outs["kp"].append(kp.reshape(bp, seq, SB_HEADS, dh)); outs["vp"].append(vp.reshape(bp, seq, SB_HEADS, dh))
            outs["ks"].append(kn.reshape(bs, dec_seq, SB_HEADS, dh)); outs["vs"].append(vn.reshape(bs, dec_seq, SB_HEADS, dh))
        xs, w_fo_c = _ffn(xs, mods_s, g_norm[i, 2], g_norm[i, 3], w_ffn_in, i, w_ffn_out, i, emit_copy=True)
        xp = _ffn(xp, mods_p, g_norm[i, 2], g_norm[i, 3], w_ffn_in, i, w_fo_c, 0)

    st = lambda name: jnp.stack(outs[name])
    return (xp, xs.reshape(bs, dec_seq, D), st("kp"), st("vp"), st("ks"), st("vs"),
            st("Cp"), st("np"), st("mp"), st("Cs"), st("ns"), st("ms"))
```
